```python
import math
import jax
import jax.numpy as jnp
from jax import lax
import numpy as np

D_MODEL = 1024
BATCH = 8
SEQ = 4096
DEPTH = 2

N_META = 16
D_FF = 4 * D_MODEL
GDN_HEADS = 4
GDN_DK = 128
GDN_DV = 128
GDN_CONV = 4
GDN_CHUNK = 64
SB_HEADS = 8
SB_DH = 64
SB_BLOCK = 128
HG_HEADS = 8
HG_DK = D_MODEL // HG_HEADS
HG_DV = D_MODEL // HG_HEADS
HG_CHUNK = 64

N_EVEN = (DEPTH + 1) // 2
N_ODD = DEPTH // 2
DN_ALPHA = float((2 * DEPTH) ** 0.25)
DN_BETA = float((8 * DEPTH) ** -0.25)
LN_EPS = 1e-5
RMS_EPS = 1e-6
L2_EPS = 1e-6

GDN_QK_W = GDN_HEADS * GDN_DK
GDN_V_W = GDN_HEADS * GDN_DV
SB_W = SB_HEADS * SB_DH
AB_SPLITS = (GDN_QK_W, GDN_QK_W, GDN_V_W, GDN_V_W, GDN_HEADS, GDN_HEADS, SB_W, SB_W, SB_W)
AB_IN = sum(AB_SPLITS)
AB_OUT_IN = GDN_V_W + SB_W
HG_K_W = HG_HEADS * HG_DK
HG_V_W = HG_HEADS * HG_DV
C_SPLITS = (HG_K_W, HG_K_W, HG_V_W, HG_V_W)
C_IN = sum(C_SPLITS)

kernel_name = "hybrid_gdn_stickbreak_hgrn2_deepnorm"


def _split(a, sizes):
    idx = np.cumsum(sizes)[:-1].tolist()
    return jnp.split(a, idx, axis=-1)


def _front_pad(a, pad):
    return jnp.pad(a, [(0, 0), (pad, 0)] + [(0, 0)] * (a.ndim - 2))


def _to_chunks(a, n, c):
    bn, _, h, d = a.shape
    return jnp.transpose(a.reshape(bn, n, c, h, d), (1, 0, 3, 2, 4))


def _from_chunks(o):
    n, bn, h, c, d = o.shape
    return jnp.transpose(o, (1, 0, 3, 2, 4)).reshape(bn, n * c, h, d)


def layer_norm(x, g, b):
    xf = x.astype(jnp.float32)
    mu = jnp.mean(xf, axis=-1, keepdims=True)
    var = jnp.mean(jnp.square(xf - mu), axis=-1, keepdims=True)
    y = (xf - mu) * lax.rsqrt(var + LN_EPS)
    return (y * g.astype(jnp.float32) + b.astype(jnp.float32)).astype(x.dtype)


def rms_norm(x, g):
    xf = x.astype(jnp.float32)
    y = xf * lax.rsqrt(jnp.mean(xf * xf, axis=-1, keepdims=True) + RMS_EPS)
    return y * g.astype(jnp.float32)


def l2_normalize(x):
    xf = x.astype(jnp.float32)
    return xf * lax.rsqrt(jnp.sum(xf * xf, axis=-1, keepdims=True) + L2_EPS)


def causal_depthwise_conv(x, w):
    k_w, ch = w.shape
    return lax.conv_general_dilated(
        x, w[:, None, :].astype(x.dtype), window_strides=(1,), padding=[(k_w - 1, 0)],
        dimension_numbers=("NWC", "WIO", "NWC"), feature_group_count=ch)


def gated_delta_rule_chunked(q, k, v, g, beta):
    bn, t_len, h, dk = q.shape
    dv = v.shape[-1]
    c = GDN_CHUNK
    n = t_len // c
    q, k, v = (_to_chunks(a, n, c) for a in (q, k, v))
    g, beta = (_to_chunks(a[..., None], n, c)[..., 0] for a in (g, beta))
    gc = jnp.cumsum(g, axis=-1)
    causal = jnp.tril(jnp.ones((c, c), dtype=bool))
    strict = jnp.tril(jnp.ones((c, c), dtype=bool), -1)
    decay = jnp.exp(jnp.where(causal, gc[..., :, None] - gc[..., None, :], -jnp.inf))
    kb = k * beta[..., None]
    m = jnp.where(strict, jnp.einsum("nbhid,nbhjd->nbhij", kb, k) * decay, 0.0)
    eye = jnp.broadcast_to(jnp.eye(c, dtype=jnp.float32), m.shape)
    t_inv = lax.linalg.triangular_solve(eye + m, eye, left_side=True, lower=True, unit_diagonal=True)
    u = jnp.einsum("nbhij,nbhjd->nbhid", t_inv, v * beta[..., None])
    w = jnp.einsum("nbhij,nbhjd->nbhid", t_inv, kb * jnp.exp(gc)[..., None])
    a_intra = jnp.einsum("nbhid,nbhjd->nbhij", q, k) * decay
    q_dec = q * jnp.exp(gc)[..., None]
    k_dec = k * jnp.exp(gc[..., -1:] - gc)[..., None]
    g_tot = jnp.exp(gc[..., -1])

    def step(s, xs):
        u_n, w_n, qd_n, kd_n, a_n, gt_n = xs
        v_new = u_n - jnp.einsum("bhcd,bhde->bhce", w_n, s)
        o_n = jnp.einsum("bhcd,bhde->bhce", qd_n, s) + jnp.einsum("bhij,bhje->bhie", a_n, v_new)
        s = s * gt_n[..., None, None] + jnp.einsum("bhcd,bhce->bhde", kd_n, v_new)
        return s, o_n

    s0 = jnp.zeros((bn, h, dk, dv), jnp.float32)
    _, o = lax.scan(step, s0, (u, w, q_dec, k_dec, a_intra, g_tot))
    return _from_chunks(o)


def stick_breaking_attention(q, k, v):
    bn, l_len, h, d = q.shape
    pad = (-l_len) % SB_BLOCK
    t_len = l_len + pad
    nb = t_len // SB_BLOCK
    to_bhtd = lambda a: jnp.moveaxis(_front_pad(a.astype(jnp.float32), pad), 1, 2)
    q, k, v = (to_bhtd(a) for a in (q, k, v))
    q = q * (d ** -0.5)
    key_pos = jnp.arange(t_len)
    q_blocks = jnp.moveaxis(q.reshape(bn, h, nb, SB_BLOCK, d), 2, 0)
    q_pos = jnp.arange(t_len).reshape(nb, SB_BLOCK)

    def block(args):
        qb, qp = args
        z = jnp.einsum("bhqd,bhkd->bhqk", qb, k)
        allowed = (key_pos[None, :] < qp[:, None]) & (key_pos[None, :] >= pad)
        log_1m = jnp.where(allowed, jax.nn.log_sigmoid(-z), 0.0)
        after = lax.cumsum(log_1m, axis=3, reverse=True) - log_1m
        wgt = jnp.where(allowed, jnp.exp(jax.nn.log_sigmoid(z) + after), 0.0)
        return jnp.einsum("bhqk,bhkd->bhqd", wgt, v)

    o = lax.map(block, (q_blocks, q_pos))
    o = jnp.moveaxis(o, 0, 2).reshape(bn, h, t_len, d)[:, :, pad:]
    return jnp.moveaxis(o, 1, 2).reshape(bn, l_len, h * d)


def hgrn2_chunked(q, k, v, logf):
    bn, t_len, h, dk = q.shape
    dv = v.shape[-1]
    c = HG_CHUNK
    n = t_len // c
    q, k, v, logf = (_to_chunks(a, n, c) for a in (q, k, v, logf))
    b = jnp.cumsum(logf, axis=3)
    causal = jnp.tril(jnp.ones((c, c), dtype=bool))[:, :, None]

    def step(s, xs):
        q_n, k_n, v_n, b_n = xs
        pair = jnp.exp(jnp.where(causal, b_n[:, :, :, None, :] - b_n[:, :, None, :, :], -jnp.inf))
        a_n = jnp.einsum("bhtc,bhsc,bhtsc->bhts", q_n, k_n, pair)
        b_last = b_n[:, :, -1:, :]
        o_n = (jnp.einsum("bhtc,bhce->bhte", q_n * jnp.exp(b_n), s)
               + jnp.einsum("bhts,bhse->bhte", a_n, v_n))
        s = (jnp.exp(b_last)[:, :, 0, :, None] * s
             + jnp.einsum("bhsc,bhse->bhce", k_n * jnp.exp(b_last - b_n), v_n))
        return s, o_n

    s0 = jnp.zeros((bn, h, dk, dv), jnp.float32)
    _, o = lax.scan(step, s0, (q, k, v, b))
    return _from_chunks(o)


def mixer_gdn_sb(h, w_in, conv_w, a_log, dt_bias, gnorm_g, w_out):
    bn, l_len, _ = h.shape
    qa, ka, va, za, ba, aa, qb, kb, vb = _split(h @ w_in, AB_SPLITS)
    qkv = jax.nn.silu(causal_depthwise_conv(jnp.concatenate([qa, ka, va], axis=-1), conv_w))
    qa, ka, va = _split(qkv, (GDN_QK_W, GDN_QK_W, GDN_V_W))
    qa = l2_normalize(qa.reshape(bn, l_len, GDN_HEADS, GDN_DK)) * (GDN_DK ** -0.5)
    ka = l2_normalize(ka.reshape(bn, l_len, GDN_HEADS, GDN_DK))
    va = va.reshape(bn, l_len, GDN_HEADS, GDN_DV).astype(jnp.float32)
    beta = jax.nn.sigmoid(ba.astype(jnp.float32))
    g = -jnp.exp(a_log.astype(jnp.float32)) * jax.nn.softplus(aa.astype(jnp.float32) + dt_bias.astype(jnp.float32))
    pad = (-l_len) % GDN_CHUNK
    oa = gated_delta_rule_chunked(*(_front_pad(a, pad) for a in (qa, ka, va, g, beta)))[:, pad:]
    oa = rms_norm(oa, gnorm_g) * jax.nn.silu(za.reshape(bn, l_len, GDN_HEADS, GDN_DV).astype(jnp.float32))
    oa = oa.reshape(bn, l_len, GDN_V_W)
    shp = (bn, l_len, SB_HEADS, SB_DH)
    ob = stick_breaking_attention(qb.reshape(shp), kb.reshape(shp), vb.reshape(shp))
    return jnp.concatenate([oa.astype(h.dtype), ob.astype(h.dtype)], axis=-1) @ w_out


def mixer_hgrn2(h, w_in, lb, gnorm_g, w_out):
    bn, l_len, _ = h.shape
    q, f, i, z = _split(h @ w_in, C_SPLITS)
    shp_k = (bn, l_len, HG_HEADS, HG_DK)
    shp_v = (bn, l_len, HG_HEADS, HG_DV)
    lbf = lb.astype(jnp.float32).reshape(HG_HEADS, HG_DK)
    fgate = lbf + (1.0 - lbf) * jax.nn.sigmoid(f.astype(jnp.float32).reshape(shp_k))
    logf = jnp.log(fgate)
    k = 1.0 - fgate
    q = jax.nn.silu(q.astype(jnp.float32)).reshape(shp_k)
    v = i.astype(jnp.float32).reshape(shp_v)
    pad = (-l_len) % HG_CHUNK
    o = hgrn2_chunked(*(_front_pad(a, pad) for a in (q, k, v, logf)))[:, pad:]
    o = rms_norm(o, gnorm_g) * jax.nn.silu(z.astype(jnp.float32).reshape(shp_v))
    return o.reshape(bn, l_len, HG_V_W).astype(h.dtype) @ w_out


def squared_relu_mlp(h, w1, w2):
    return jnp.square(jax.nn.relu(h @ w1)) @ w2


def setup_inputs(seed: int = 0) -> dict:
    key = jax.random.key(seed)
    ks = jax.random.split(key, 18)
    nrm = lambda k, shape, scale: jax.random.normal(k, shape, jnp.float32) * scale
    x = nrm(ks[0], (BATCH, SEQ, D_MODEL), 1.0)
    meta_tokens = nrm(ks[1], (N_META, D_MODEL), 1.0)
    ab_w_in = nrm(ks[2], (N_EVEN, D_MODEL, AB_IN), D_MODEL ** -0.5)
    ab_conv_w = nrm(ks[3], (N_EVEN, GDN_CONV, 2 * GDN_QK_W + GDN_V_W), GDN_CONV ** -0.5)
    ab_a_log = jnp.log(jax.random.uniform(ks[4], (N_EVEN, GDN_HEADS), jnp.float32, 1.0, 16.0))
    dt = jnp.exp(jax.random.uniform(ks[5], (N_EVEN, GDN_HEADS), jnp.float32, math.log(1e-3), math.log(1e-1)))
    ab_dt_bias = dt + jnp.log(-jnp.expm1(-dt))
    ab_gnorm_g = 1.0 + nrm(ks[6], (N_EVEN, GDN_DV), 0.02)
    ab_w_out = nrm(ks[7], (N_EVEN, AB_OUT_IN, D_MODEL), (AB_OUT_IN ** -0.5) * DN_BETA)
    c_w_in = nrm(ks[8], (N_ODD, D_MODEL, C_IN), D_MODEL ** -0.5)
    c_lb_raw = nrm(ks[9], (DEPTH, HG_K_W), 0.1)
    c_gnorm_g = 1.0 + nrm(ks[10], (N_ODD, HG_DV), 0.02)
    c_w_out = nrm(ks[11], (N_ODD, HG_V_W, D_MODEL), (HG_V_W ** -0.5) * DN_BETA)
    ln_mix_g = 1.0 + nrm(ks[12], (DEPTH, D_MODEL), 0.02)
    ln_mix_b = nrm(ks[13], (DEPTH, D_MODEL), 0.02)
    mlp_w1 = nrm(ks[14], (DEPTH, D_MODEL, D_FF), D_MODEL ** -0.5)
    mlp_w2 = nrm(ks[15], (DEPTH, D_FF, D_MODEL), (D_FF ** -0.5) * DN_BETA)
    ln_ffn_g = 1.0 + nrm(ks[16], (DEPTH, D_MODEL), 0.02)
    ln_ffn_b = nrm(ks[17], (DEPTH, D_MODEL), 0.02)
    return {"x": x, "meta_tokens": meta_tokens, "ab_w_in": ab_w_in, "ab_conv_w": ab_conv_w,
            "ab_a_log": ab_a_log, "ab_dt_bias": ab_dt_bias, "ab_gnorm_g": ab_gnorm_g, "ab_w_out": ab_w_out,
            "c_w_in": c_w_in, "c_lb_raw": c_lb_raw, "c_gnorm_g": c_gnorm_g, "c_w_out": c_w_out,
            "ln_mix_g": ln_mix_g, "ln_mix_b": ln_mix_b, "mlp_w1": mlp_w1, "mlp_w2": mlp_w2,
            "ln_ffn_g": ln_ffn_g, "ln_ffn_b": ln_ffn_b}


def reference(x, meta_tokens, ab_w_in, ab_conv_w, ab_a_log, ab_dt_bias, ab_gnorm_g, ab_w_out,
              c_w_in, c_lb_raw, c_gnorm_g, c_w_out, ln_mix_g, ln_mix_b, mlp_w1, mlp_w2,
              ln_ffn_g, ln_ffn_b):
    lb_all = jnp.cumsum(jax.nn.softmax(c_lb_raw.astype(jnp.float32), axis=0), axis=0)
    lb_all = lb_all - lb_all[0:1]
    meta = jnp.broadcast_to(meta_tokens[None].astype(x.dtype), (x.shape[0], N_META, D_MODEL))
    h = jnp.concatenate([meta, x], axis=1)
    for layer in range(DEPTH):
        j = layer // 2
        if layer % 2 == 0:
            mix = mixer_gdn_sb(h, ab_w_in[j], ab_conv_w[j], ab_a_log[j], ab_dt_bias[j], ab_gnorm_g[j], ab_w_out[j])
        else:
            mix = mixer_hgrn2(h, c_w_in[j], lb_all[layer], c_gnorm_g[j], c_w_out[j])
        h = layer_norm(DN_ALPHA * h + mix, ln_mix_g[layer], ln_mix_b[layer])
        h = layer_norm(DN_ALPHA * h + squared_relu_mlp(h, mlp_w1[layer], mlp_w2[layer]),
                       ln_ffn_g[layer], ln_ffn_b[layer])
    return h[:, N_META:]
```

```python
import functools
import math

import jax
import jax.numpy as jnp
from jax import lax
from jax.experimental import pallas as pl
from jax.experimental.pallas import tpu as pltpu

F32 = jnp.float32
BF16 = jnp.bfloat16
HIGHEST = lax.Precision.HIGHEST

BLK = 128
LN_EPS = 1e-5
RMS_EPS = 1e-6
L2_EPS = 1e-6
GDN_HEADS = 4
GDN_D = 128
GDN_CONV = 4
SB_HEADS = 8
SB_DH = 64
HG_HEADS = 8
HG_D = 128
VMEM_LIMIT = 56 * 1024 * 1024


def _cparams(sem):
    return pltpu.CompilerParams(dimension_semantics=sem, vmem_limit_bytes=VMEM_LIMIT)


def _dot(a, b):
    return jnp.dot(a, b, preferred_element_type=F32)


def _dot_nt(a, b):
    return lax.dot_general(a, b, (((1,), (1,)), ((), ())), preferred_element_type=F32)


def _dot_tn(a, b):
    return lax.dot_general(a, b, (((0,), (0,)), ((), ())), preferred_element_type=F32)


def _dot_exact(a, b):
    return jnp.dot(a, b, preferred_element_type=F32, precision=HIGHEST)


def _sigmoid(x):
    return 1.0 / (1.0 + jnp.exp(-x))


def _silu(x):
    return x * _sigmoid(x)


def _softplus(x):
    return jnp.maximum(x, 0.0) + jnp.log(1.0 + jnp.exp(-jnp.abs(x)))


def _iota2(shape, dim):
    return lax.broadcasted_iota(jnp.int32, shape, dim)


def _layer_norm(x, g, b):
    mu = jnp.mean(x, axis=-1, keepdims=True)
    xc = x - mu
    var = jnp.mean(xc * xc, axis=-1, keepdims=True)
    return xc * lax.rsqrt(var + LN_EPS) * g + b


def _proj_kernel(x_ref, *refs, n_out):
    w_refs, o_refs = refs[:n_out], refs[n_out:]
    xb = x_ref[...].astype(BF16)
    for w_ref, o_ref in zip(w_refs, o_refs):
        o_ref[...] = _dot(xb, w_ref[...]).astype(o_ref.dtype)


def _project(x2d, weights, out_dtypes, tm):
    m, d = x2d.shape
    n_out = len(weights)
    in_specs = [pl.BlockSpec((tm, d), lambda i: (i, 0))]
    for w in weights:
        in_specs.append(pl.BlockSpec(w.shape, lambda i: (0, 0), pipeline_mode=pl.Buffered(1)))
    out_specs = [pl.BlockSpec((tm, w.shape[1]), lambda i: (i, 0)) for w in weights]
    out_shape = [jax.ShapeDtypeStruct((m, w.shape[1]), dt) for w, dt in zip(weights, out_dtypes)]
    return pl.pallas_call(
        functools.partial(_proj_kernel, n_out=n_out),
        grid=(m // tm,),
        in_specs=in_specs,
        out_specs=out_specs,
        out_shape=out_shape,
        compiler_params=_cparams(("parallel",)),
        name="proj",
    )(x2d, *weights)


def _post_kernel(*refs, n_mix, alpha, ff_chunks):
    y_refs = refs[:n_mix]
    wo_refs = refs[n_mix:2 * n_mix]
    h_ref, g1_ref, b1_ref, w1_ref, w2_ref, g2_ref, b2_ref, o_ref = refs[2 * n_mix:]
    mix = _dot(y_refs[0][...], wo_refs[0][...])
    for y_ref, wo_ref in zip(y_refs[1:], wo_refs[1:]):
        mix = mix + _dot(y_ref[...], wo_ref[...])
    h1 = _layer_norm(alpha * h_ref[...] + mix, g1_ref[...], b1_ref[...])
    h1b = h1.astype(BF16)
    d_ff = w1_ref.shape[1]
    fc = d_ff // ff_chunks
    acc = None
    for j in range(ff_chunks):
        hid = _dot(h1b, w1_ref[:, j * fc:(j + 1) * fc])
        hid = jnp.square(jnp.maximum(hid, 0.0)).astype(BF16)
        part = _dot(hid, w2_ref[j * fc:(j + 1) * fc, :])
        acc = part if acc is None else acc + part
    o_ref[...] = _layer_norm(alpha * h1 + acc, g2_ref[...], b2_ref[...])


def _post_mixer(ys, wos, h2d, g1, b1, w1, w2, g2, b2, alpha, tm):
    m, d = h2d.shape
    n_mix = len(ys)
    row = lambda i: (i, 0)
    fixed = lambda i: (0, 0)
    const = lambda a: pl.BlockSpec(a.shape, fixed, pipeline_mode=pl.Buffered(1))
    in_specs = [pl.BlockSpec((tm, y.shape[1]), row) for y in ys]
    in_specs += [const(w) for w in wos]
    in_specs += [pl.BlockSpec((tm, d), row), const(g1), const(b1), const(w1), const(w2), const(g2), const(b2)]
    return pl.pallas_call(
        functools.partial(_post_kernel, n_mix=n_mix, alpha=alpha, ff_chunks=4),
        grid=(m // tm,),
        in_specs=in_specs,
        out_specs=pl.BlockSpec((tm, d), row),
        out_shape=jax.ShapeDtypeStruct((m, d), F32),
        compiler_params=_cparams(("parallel",)),
        name="post_mixer",
    )(*ys, *wos, h2d, g1, b1, w1, w2, g2, b2)


def _unit_lower_inverse(m_strict):
    c = m_strict.shape[0]
    eye = (_iota2((c, c), 0) == _iota2((c, c), 1)).astype(F32)
    x = -m_strict
    p = eye + x
    for _ in range(int(math.log2(c)) - 1):
        xb = x.astype(BF16)
        x = _dot(xb, xb)
        p = p + _dot(p.astype(BF16), x.astype(BF16))
    return p


def _gdn_kernel(qkv_ref, z_ref, gate_ref, convw_ref, alog_ref, dtb_ref, gn_ref, o_ref, xc_ref, s_ref):
    c = BLK
    nh = GDN_HEADS
    w_qkv = nh * GDN_D * 3

    @pl.when(pl.program_id(1) == 0)
    def _():
        xc_ref[0:8, :] = jnp.zeros((8, w_qkv), F32)
        s_ref[...] = jnp.zeros_like(s_ref)

    xc_ref[8:8 + c, :] = qkv_ref[...]
    conv = None
    for k in range(GDN_CONV):
        off = 8 - (GDN_CONV - 1) + k
        term = xc_ref[off:off + c, :] * convw_ref[k:k + 1, :]
        conv = term if conv is None else conv + term
    xc_ref[0:8, :] = xc_ref[c:c + 8, :]
    qkv = _silu(conv)

    gate = gate_ref[...]
    beta_all = _sigmoid(gate)
    g_all = -jnp.exp(alog_ref[...]) * _softplus(gate + dtb_ref[...])
    row = _iota2((c, c), 0)
    col = _iota2((c, c), 1)
    tril = (row >= col).astype(F32)
    gc_all = _dot_exact(tril, g_all)
    gc_all_t = gc_all.T
    causal = row >= col
    strict = row > col

    for h in range(nh):
        q = qkv[:, h * GDN_D:(h + 1) * GDN_D]
        k = qkv[:, (nh + h) * GDN_D:(nh + h + 1) * GDN_D]
        v = qkv[:, (2 * nh + h) * GDN_D:(2 * nh + h + 1) * GDN_D]
        q = q * lax.rsqrt(jnp.sum(q * q, axis=-1, keepdims=True) + L2_EPS) * (GDN_D ** -0.5)
        k = k * lax.rsqrt(jnp.sum(k * k, axis=-1, keepdims=True) + L2_EPS)
        beta = beta_all[:, h:h + 1]
        gc = gc_all[:, nh + h:nh + h + 1]
        gc_row = gc_all_t[nh + h:nh + h + 1, :]
        gc_last = gc_all[c - 1:c, nh + h:nh + h + 1]
        decay = jnp.where(causal, jnp.exp(jnp.minimum(gc - gc_row, 0.0)), 0.0)
        kb = k * beta
        kbf = k.astype(BF16)
        m_strict = jnp.where(strict, _dot_nt(kb.astype(BF16), kbf) * decay, 0.0)
        t_inv = _unit_lower_inverse(m_strict).astype(BF16)
        egc = jnp.exp(gc)
        u = _dot(t_inv, (v * beta).astype(BF16))
        w = _dot(t_inv, (kb * egc).astype(BF16))
        a_intra = _dot_nt(q.astype(BF16), kbf) * decay
        q_dec = (q * egc).astype(BF16)
        k_dec = (k * jnp.exp(gc_last - gc)).astype(BF16)
        s = s_ref[h]
        sb = s.astype(BF16)
        v_new = u - _dot(w.astype(BF16), sb)
        v_newb = v_new.astype(BF16)
        o = _dot(q_dec, sb) + _dot(a_intra.astype(BF16), v_newb)
        s_ref[h] = s * jnp.exp(gc_last) + _dot_tn(k_dec, v_newb)
        o = o * lax.rsqrt(jnp.mean(o * o, axis=-1, keepdims=True) + RMS_EPS) * gn_ref[...]
        zh = z_ref[:, h * GDN_D:(h + 1) * GDN_D]
        o_ref[:, h * GDN_D:(h + 1) * GDN_D] = (o * _silu(zh)).astype(o_ref.dtype)


def _gdn(gdn_in, gate, conv_w, alog_row, dtb_row, gn_row):
    b, t, _ = gdn_in.shape
    nc = t // BLK
    w_qkv = GDN_HEADS * GDN_D * 3
    w_v = GDN_HEADS * GDN_D
    fixed = lambda i, j: (0, 0)
    return pl.pallas_call(
        _gdn_kernel,
        grid=(b, nc),
        in_specs=[
            pl.BlockSpec((None, BLK, w_qkv), lambda i, j: (i, j, 0)),
            pl.BlockSpec((None, BLK, w_v), lambda i, j: (i, j, 3)),
            pl.BlockSpec((None, BLK, BLK), lambda i, j: (i, j, 0)),
            pl.BlockSpec(conv_w.shape, fixed),
            pl.BlockSpec(alog_row.shape, fixed),
            pl.BlockSpec(dtb_row.shape, fixed),
            pl.BlockSpec(gn_row.shape, fixed),
        ],
        out_specs=pl.BlockSpec((None, BLK, w_v), lambda i, j: (i, j, 0)),
        out_shape=jax.ShapeDtypeStruct((b, t, w_v), BF16),
        scratch_shapes=[
            pltpu.VMEM((BLK + 8, w_qkv), F32),
            pltpu.VMEM((GDN_HEADS, GDN_D, GDN_D), F32),
        ],
        compiler_params=_cparams(("parallel", "arbitrary")),
        name="gdn",
    )(gdn_in, gdn_in, gate, conv_w, alog_row, dtb_row, gn_row)


def _sb_kernel(q_ref, k_ref, v_ref, o_ref, *, pad):
    c = BLK
    qi = pl.program_id(2)
    lane = _iota2((c, c), 1)
    row = _iota2((c, c), 0)
    upper_strict = (row > lane).astype(BF16)
    qv = q_ref[...] * jnp.asarray(SB_DH ** -0.5, BF16)
    heads = []
    for hh in range(2):
        in_head = (lane >= hh * SB_DH) & (lane < (hh + 1) * SB_DH)
        heads.append(jnp.where(in_head, qv, jnp.zeros_like(qv)))
    q_pos = qi * c + row

    def body(step, carry):
        j = qi - step
        start = pl.multiple_of(j * c, c)
        kj = k_ref[pl.ds(start, c), :]
        vj = v_ref[pl.ds(start, c), :]
        key_pos = j * c + lane
        allowed = (key_pos < q_pos) & (key_pos >= pad)
        out = []
        for hh in range(2):
            acc, run = carry[2 * hh], carry[2 * hh + 1]
            z = _dot_nt(heads[hh], kj)
            log_1m = jnp.where(allowed, -_softplus(z), 0.0)
            hi = log_1m.astype(BF16)
            lo = (log_1m - hi.astype(F32)).astype(BF16)
            after = _dot(hi, upper_strict) + _dot(lo, upper_strict) + run
            wgt = jnp.where(allowed, jnp.exp(z + log_1m + after), 0.0)
            acc = acc + _dot(wgt.astype(BF16), vj)
            run = run + jnp.sum(log_1m, axis=-1, keepdims=True)
            out += [acc, run]
        return tuple(out)

    zero_acc = jnp.zeros((c, c), F32)
    zero_run = jnp.zeros((c, 1), F32)
    res = lax.fori_loop(0, qi + 1, body, (zero_acc, zero_run, zero_acc, zero_run))
    o_ref[...] = jnp.where(lane < SB_DH, res[0], res[2]).astype(o_ref.dtype)


def _stick_breaking(sb_in, pad):
    b, t, _ = sb_in.shape
    nb = t // BLK
    pairs = SB_HEADS * SB_DH // BLK
    return pl.pallas_call(
        functools.partial(_sb_kernel, pad=pad),
        grid=(b, pairs, nb),
        in_specs=[
            pl.BlockSpec((None, BLK, BLK), lambda i, p, j: (i, j, p)),
            pl.BlockSpec((None, t, BLK), lambda i, p, j: (i, 0, pairs + p)),
            pl.BlockSpec((None, t, BLK), lambda i, p, j: (i, 0, 2 * pairs + p)),
        ],
        out_specs=pl.BlockSpec((None, BLK, BLK), lambda i, p, j: (i, j, p)),
        out_shape=jax.ShapeDtypeStruct((b, t, pairs * BLK), BF16),
        compiler_params=_cparams(("parallel", "parallel", "arbitrary")),
        name="stick_breaking",
    )(sb_in, sb_in, sb_in)


def _row_in_group(x, idx, group):
    c, d = x.shape
    x3 = x.reshape(c // group, group, d)
    return jnp.broadcast_to(x3[:, idx:idx + 1, :], x3.shape).reshape(c, d)


def _hgrn_intra(q, k, b):
    c = q.shape[0]
    row = _iota2((c, c), 0)
    col = _iota2((c, c), 1)
    a = jnp.zeros((c, c), F32)
    half = c // 2
    while half >= 8:
        ref_row = _row_in_group(b, half, 2 * half)
        lower = (row % (2 * half)) >= half
        e = jnp.exp(jnp.where(lower, b - ref_row, ref_row - b))
        ql = jnp.where(lower, q * e, 0.0).astype(BF16)
        kl = jnp.where(lower, 0.0, k * e).astype(BF16)
        same = (row // (2 * half)) == (col // (2 * half))
        a = a + jnp.where(same, _dot_nt(ql, kl), 0.0)
        half //= 2
    grp = row // 8 * 8
    for s in range(8):
        ks = _row_in_group(k, s, 8)
        bs = _row_in_group(b, s, 8)
        p = q * ks * jnp.exp(jnp.minimum(b - bs, 0.0))
        tot = jnp.sum(p, axis=-1, keepdims=True)
        hit = (col == grp + s) & (row % 8 >= s)
        a = jnp.where(hit, tot, a)
    return a


def _hgrn_kernel(q_ref, f_ref, i_ref, z_ref, lbraw_ref, gn_ref, o_ref, s_ref, *, pad, layer, heads):
    c = BLK
    j = pl.program_id(2)

    @pl.when(j == 0)
    def _():
        s_ref[...] = jnp.zeros_like(s_ref)

    row = _iota2((c, c), 0)
    col = _iota2((c, c), 1)
    tril = (row >= col).astype(F32)
    real = (j * c + row) >= pad
    for h in range(heads):
        sl = slice(h * HG_D, (h + 1) * HG_D)
        raw = lbraw_ref[:, sl]
        ex = jnp.exp(raw - jnp.max(raw, axis=0, keepdims=True))
        sm = ex / jnp.sum(ex, axis=0, keepdims=True)
        lb = jnp.sum(sm[1:layer + 1, :], axis=0, keepdims=True)
        q = _silu(q_ref[:, sl])
        fgate = lb + (1.0 - lb) * _sigmoid(f_ref[:, sl])
        logf = jnp.where(real, jnp.log(fgate), 0.0)
        k = jnp.where(real, 1.0 - fgate, 0.0)
        v = jnp.where(real, i_ref[:, sl], 0.0)
        vb = v.astype(BF16)
        b = _dot_exact(tril, logf)
        b_last = b[c - 1:c, :]
        s_t = s_ref[h]
        o = _dot_nt((q * jnp.exp(b)).astype(BF16), s_t.astype(BF16))
        a = _hgrn_intra(q, k, b)
        o = o + _dot(a.astype(BF16), vb)
        k_dec = (k * jnp.exp(b_last - b)).astype(BF16)
        s_ref[h] = s_t * jnp.exp(b_last) + _dot_tn(vb, k_dec)
        o = o * lax.rsqrt(jnp.mean(o * o, axis=-1, keepdims=True) + RMS_EPS) * gn_ref[...]
        o_ref[:, sl] = (o * _silu(z_ref[:, sl])).astype(o_ref.dtype)


def _hgrn2(c_in, lb_raw, gn_row, pad, layer, heads_per_step=2):
    b, t, _ = c_in.shape
    nc = t // BLK
    groups = HG_HEADS // heads_per_step
    w = heads_per_step * HG_D
    depth = lb_raw.shape[0]
    col_spec = lambda part: pl.BlockSpec((None, BLK, w), lambda i, g, j: (i, j, part * groups + g))
    return pl.pallas_call(
        functools.partial(_hgrn_kernel, pad=pad, layer=layer, heads=heads_per_step),
        grid=(b, groups, nc),
        in_specs=[
            col_spec(0), col_spec(1), col_spec(2), col_spec(3),
            pl.BlockSpec((depth, w), lambda i, g, j: (0, g)),
            pl.BlockSpec(gn_row.shape, lambda i, g, j: (0, 0)),
        ],
        out_specs=pl.BlockSpec((None, BLK, w), lambda i, g, j: (i, j, g)),
        out_shape=jax.ShapeDtypeStruct((b, t, HG_HEADS * HG_D), BF16),
        scratch_shapes=[pltpu.VMEM((heads_per_step, HG_D, HG_D), F32)],
        compiler_params=_cparams(("parallel", "parallel", "arbitrary")),
        name="hgrn2",
    )(c_in, c_in, c_in, c_in, lb_raw, gn_row)


def _lane_row(vec, offset, width=BLK):
    return jnp.zeros((1, width), F32).at[0, offset:offset + vec.shape[0]].set(vec.astype(F32))


def kernel(x, meta_tokens, ab_w_in, ab_conv_w, ab_a_log, ab_dt_bias, ab_gnorm_g, ab_w_out, c_w_in, c_lb_raw,
           c_gnorm_g, c_w_out, ln_mix_g, ln_mix_b, mlp_w1, mlp_w2, ln_ffn_g, ln_ffn_b):
    bsz, seq, d = x.shape
    n_meta = meta_tokens.shape[0]
    depth = ln_mix_g.shape[0]
    alpha = float((2 * depth) ** 0.25)
    pad = (-n_meta) % BLK
    t = pad + n_meta + seq
    assert seq % BLK == 0 and depth == 2
    m = bsz * t
    tm = 512
    assert m % tm == 0

    meta = jnp.broadcast_to(meta_tokens[None].astype(x.dtype), (bsz, n_meta, d))
    h = jnp.concatenate([jnp.zeros((bsz, pad, d), x.dtype), meta, x], axis=1).reshape(m, d)

    w_gdn = GDN_HEADS * GDN_D * 4
    w_sb = SB_HEADS * SB_DH * 3
    row2 = lambda a: a.astype(F32).reshape(1, -1)

    w_in = ab_w_in[0]
    wa = w_in[:, :w_gdn].astype(BF16)
    wg = jnp.pad(w_in[:, w_gdn:w_gdn + 2 * GDN_HEADS], ((0, 0), (0, BLK - 2 * GDN_HEADS))).astype(BF16)
    wb = w_in[:, w_gdn + 2 * GDN_HEADS:].astype(BF16)
    gdn_in, gate, sb_in = _project(h, [wa, wg, wb], [F32, F32, BF16], tm)
    oa = _gdn(gdn_in.reshape(bsz, t, w_gdn), gate.reshape(bsz, t, BLK), ab_conv_w[0].astype(F32),
              _lane_row(ab_a_log[0], GDN_HEADS), _lane_row(ab_dt_bias[0], GDN_HEADS), row2(ab_gnorm_g[0]))
    ob = _stick_breaking(sb_in.reshape(bsz, t, w_sb), pad)
    w_out = ab_w_out[0].astype(BF16)
    n_a = GDN_HEADS * GDN_D
    h = _post_mixer([oa.reshape(m, n_a), ob.reshape(m, -1)], [w_out[:n_a], w_out[n_a:]], h,
                    row2(ln_mix_g[0]), row2(ln_mix_b[0]), mlp_w1[0].astype(BF16), mlp_w2[0].astype(BF16),
                    row2(ln_ffn_g[0]), row2(ln_ffn_b[0]), alpha, tm)

    (c_in,) = _project(h, [c_w_in[0].astype(BF16)], [F32], tm)
    oc = _hgrn2(c_in.reshape(bsz, t, -1), c_lb_raw.astype(F32), row2(c_gnorm_g[0]), pad, 1)
    h = _post_mixer([oc.reshape(m, -1)], [c_w_out[0].astype(BF16)], h,
                    row2(ln_mix_g[1]), row2(ln_mix_b[1]), mlp_w1[1].astype(BF16), mlp_w2[1].astype(BF16),
                    row2(ln_ffn_g[1]), row2(ln_ffn_b[1]), alpha, tm)
    return h.reshape(bsz, t, d)[:, pad + n_meta:]
```

```python
import functools
import math

import jax
import jax.numpy as jnp
from jax import lax
from jax.experimental import pallas as pl
from jax.experimental.pallas import tpu as pltpu

F32 = jnp.float32
BF16 = jnp.bfloat16
HIGHEST = lax.Precision.HIGHEST

BLK = 128
LN_EPS = 1e-5
RMS_EPS = 1e-6
L2_EPS = 1e-6
GDN_HEADS = 4
GDN_D = 128
GDN_CONV = 4
SB_HEADS = 8
SB_DH = 64
HG_HEADS = 8
HG_D = 128
SB_UNDERFLOW = 104.0
VMEM_LIMIT = 56 * 1024 * 1024


def _cparams(sem):
    return pltpu.CompilerParams(dimension_semantics=sem, vmem_limit_bytes=VMEM_LIMIT)


def _dot(a, b):
    return jnp.dot(a, b, preferred_element_type=F32)


def _dot_nt(a, b):
    return lax.dot_general(a, b, (((1,), (1,)), ((), ())), preferred_element_type=F32)


def _dot_tn(a, b):
    return lax.dot_general(a, b, (((0,), (0,)), ((), ())), preferred_element_type=F32)


def _dot_exact(a, b):
    return jnp.dot(a, b, preferred_element_type=F32, precision=HIGHEST)


def _sigmoid(x):
    return 1.0 / (1.0 + jnp.exp(-x))


def _silu(x):
    return x * _sigmoid(x)


def _softplus(x):
    return jnp.maximum(x, 0.0) + jnp.log(1.0 + jnp.exp(-jnp.abs(x)))


def _iota2(shape, dim):
    return lax.broadcasted_iota(jnp.int32, shape, dim)


def _layer_norm(x, g, b):
    mu = jnp.mean(x, axis=-1, keepdims=True)
    xc = x - mu
    var = jnp.mean(xc * xc, axis=-1, keepdims=True)
    return xc * lax.rsqrt(var + LN_EPS) * g + b


def _proj_kernel(x_ref, *refs, n_out):
    w_refs, o_refs = refs[:n_out], refs[n_out:]
    xb = x_ref[...].astype(BF16)
    for w_ref, o_ref in zip(w_refs, o_refs):
        o_ref[...] = _dot(xb, w_ref[...]).astype(o_ref.dtype)


def _project(x2d, weights, out_dtypes, tm):
    m, d = x2d.shape
    n_out = len(weights)
    in_specs = [pl.BlockSpec((tm, d), lambda i: (i, 0))]
    for w in weights:
        in_specs.append(pl.BlockSpec(w.shape, lambda i: (0, 0), pipeline_mode=pl.Buffered(1)))
    out_specs = [pl.BlockSpec((tm, w.shape[1]), lambda i: (i, 0)) for w in weights]
    out_shape = [jax.ShapeDtypeStruct((m, w.shape[1]), dt) for w, dt in zip(weights, out_dtypes)]
    return pl.pallas_call(
        functools.partial(_proj_kernel, n_out=n_out),
        grid=(m // tm,),
        in_specs=in_specs,
        out_specs=out_specs,
        out_shape=out_shape,
        compiler_params=_cparams(("parallel",)),
        name="proj",
    )(x2d, *weights)


def _post_kernel(*refs, n_mix, alpha, ff_chunks):
    y_refs = refs[:n_mix]
    wo_refs = refs[n_mix:2 * n_mix]
    h_ref, g1_ref, b1_ref, w1_ref, w2_ref, g2_ref, b2_ref, o_ref = refs[2 * n_mix:]
    mix = _dot(y_refs[0][...], wo_refs[0][...])
    for y_ref, wo_ref in zip(y_refs[1:], wo_refs[1:]):
        mix = mix + _dot(y_ref[...], wo_ref[...])
    h1 = _layer_norm(alpha * h_ref[...] + mix, g1_ref[...], b1_ref[...])
    h1b = h1.astype(BF16)
    d_ff = w1_ref.shape[1]
    fc = d_ff // ff_chunks
    acc = None
    for j in range(ff_chunks):
        hid = _dot(h1b, w1_ref[:, j * fc:(j + 1) * fc])
        hid = jnp.square(jnp.maximum(hid, 0.0)).astype(BF16)
        part = _dot(hid, w2_ref[j * fc:(j + 1) * fc, :])
        acc = part if acc is None else acc + part
    o_ref[...] = _layer_norm(alpha * h1 + acc, g2_ref[...], b2_ref[...])


def _post_mixer(ys, wos, h2d, g1, b1, w1, w2, g2, b2, alpha, tm):
    m, d = h2d.shape
    n_mix = len(ys)
    row = lambda i: (i, 0)
    fixed = lambda i: (0, 0)
    const = lambda a: pl.BlockSpec(a.shape, fixed, pipeline_mode=pl.Buffered(1))
    in_specs = [pl.BlockSpec((tm, y.shape[1]), row) for y in ys]
    in_specs += [const(w) for w in wos]
    in_specs += [pl.BlockSpec((tm, d), row), const(g1), const(b1), const(w1), const(w2), const(g2), const(b2)]
    return pl.pallas_call(
        functools.partial(_post_kernel, n_mix=n_mix, alpha=alpha, ff_chunks=4),
        grid=(m // tm,),
        in_specs=in_specs,
        out_specs=pl.BlockSpec((tm, d), row),
        out_shape=jax.ShapeDtypeStruct((m, d), F32),
        compiler_params=_cparams(("parallel",)),
        name="post_mixer",
    )(*ys, *wos, h2d, g1, b1, w1, w2, g2, b2)


def _unit_lower_inverse(m_strict):
    c = m_strict.shape[0]
    eye = (_iota2((c, c), 0) == _iota2((c, c), 1)).astype(F32)
    x = -m_strict
    p = eye + x
    for _ in range(int(math.log2(c)) - 1):
        xb = x.astype(BF16)
        x = _dot(xb, xb)
        p = p + _dot(p.astype(BF16), x.astype(BF16))
    return p


def _gdn_kernel(qkv_ref, z_ref, gate_ref, convw_ref, alog_ref, dtb_ref, gn_ref, o_ref, xc_ref, s_ref):
    c = BLK
    nh = GDN_HEADS
    w_qkv = nh * GDN_D * 3

    @pl.when(pl.program_id(1) == 0)
    def _():
        xc_ref[0:8, :] = jnp.zeros((8, w_qkv), F32)
        s_ref[...] = jnp.zeros_like(s_ref)

    xc_ref[8:8 + c, :] = qkv_ref[...]
    conv = None
    for k in range(GDN_CONV):
        off = 8 - (GDN_CONV - 1) + k
        term = xc_ref[off:off + c, :] * convw_ref[k:k + 1, :]
        conv = term if conv is None else conv + term
    xc_ref[0:8, :] = xc_ref[c:c + 8, :]
    qkv = _silu(conv)

    gate = gate_ref[...]
    beta_all = _sigmoid(gate)
    g_all = -jnp.exp(alog_ref[...]) * _softplus(gate + dtb_ref[...])
    row = _iota2((c, c), 0)
    col = _iota2((c, c), 1)
    tril = (row >= col).astype(F32)
    gc_all = _dot_exact(tril, g_all)
    gc_all_t = gc_all.T
    causal = row >= col
    strict = row > col

    for h in range(nh):
        q = qkv[:, h * GDN_D:(h + 1) * GDN_D]
        k = qkv[:, (nh + h) * GDN_D:(nh + h + 1) * GDN_D]
        v = qkv[:, (2 * nh + h) * GDN_D:(2 * nh + h + 1) * GDN_D]
        q = q * lax.rsqrt(jnp.sum(q * q, axis=-1, keepdims=True) + L2_EPS) * (GDN_D ** -0.5)
        k = k * lax.rsqrt(jnp.sum(k * k, axis=-1, keepdims=True) + L2_EPS)
        beta = beta_all[:, h:h + 1]
        gc = gc_all[:, nh + h:nh + h + 1]
        gc_row = gc_all_t[nh + h:nh + h + 1, :]
        gc_last = gc_all[c - 1:c, nh + h:nh + h + 1]
        decay = jnp.where(causal, jnp.exp(jnp.minimum(gc - gc_row, 0.0)), 0.0)
        kb = k * beta
        kbf = k.astype(BF16)
        m_strict = jnp.where(strict, _dot_nt(kb.astype(BF16), kbf) * decay, 0.0)
        t_inv = _unit_lower_inverse(m_strict).astype(BF16)
        egc = jnp.exp(gc)
        u = _dot(t_inv, (v * beta).astype(BF16))
        w = _dot(t_inv, (kb * egc).astype(BF16))
        a_intra = _dot_nt(q.astype(BF16), kbf) * decay
        q_dec = (q * egc).astype(BF16)
        k_dec = (k * jnp.exp(gc_last - gc)).astype(BF16)
        s = s_ref[h]
        sb = s.astype(BF16)
        v_new = u - _dot(w.astype(BF16), sb)
        v_newb = v_new.astype(BF16)
        o = _dot(q_dec, sb) + _dot(a_intra.astype(BF16), v_newb)
        s_ref[h] = s * jnp.exp(gc_last) + _dot_tn(k_dec, v_newb)
        o = o * lax.rsqrt(jnp.mean(o * o, axis=-1, keepdims=True) + RMS_EPS) * gn_ref[...]
        zh = z_ref[:, h * GDN_D:(h + 1) * GDN_D]
        o_ref[:, h * GDN_D:(h + 1) * GDN_D] = (o * _silu(zh)).astype(o_ref.dtype)


def _gdn(gdn_in, gate, conv_w, alog_row, dtb_row, gn_row):
    b, t, _ = gdn_in.shape
    nc = t // BLK
    w_qkv = GDN_HEADS * GDN_D * 3
    w_v = GDN_HEADS * GDN_D
    fixed = lambda i, j: (0, 0)
    return pl.pallas_call(
        _gdn_kernel,
        grid=(b, nc),
        in_specs=[
            pl.BlockSpec((None, BLK, w_qkv), lambda i, j: (i, j, 0)),
            pl.BlockSpec((None, BLK, w_v), lambda i, j: (i, j, 3)),
            pl.BlockSpec((None, BLK, BLK), lambda i, j: (i, j, 0)),
            pl.BlockSpec(conv_w.shape, fixed),
            pl.BlockSpec(alog_row.shape, fixed),
            pl.BlockSpec(dtb_row.shape, fixed),
            pl.BlockSpec(gn_row.shape, fixed),
        ],
        out_specs=pl.BlockSpec((None, BLK, w_v), lambda i, j: (i, j, 0)),
        out_shape=jax.ShapeDtypeStruct((b, t, w_v), BF16),
        scratch_shapes=[
            pltpu.VMEM((BLK + 8, w_qkv), F32),
            pltpu.VMEM((GDN_HEADS, GDN_D, GDN_D), F32),
        ],
        compiler_params=_cparams(("parallel", "arbitrary")),
        name="gdn",
    )(gdn_in, gdn_in, gate, conv_w, alog_row, dtb_row, gn_row)


def _sb_kernel(q_ref, k_ref, v_ref, o_ref, *, pad, pairs):
    c = BLK
    qi = pl.program_id(1)
    lane2 = _iota2((2 * c, c), 1)
    rel = lane2 - (_iota2((2 * c, c), 0) & (c - 1))
    upper_strict = (_iota2((c, c), 0) > _iota2((c, c), 1)).astype(BF16)
    head0 = _iota2((c, c), 1) < SB_DH
    scale = jnp.asarray(SB_DH ** -0.5, BF16)
    zero_b = jnp.zeros((c, c), BF16)

    def split_heads(x):
        return jnp.concatenate([jnp.where(head0, x, zero_b), jnp.where(head0, zero_b, x)], axis=0)

    q_cat = [split_heads(q_ref[:, p * c:(p + 1) * c] * scale) for p in range(pairs)]

    def cond(carry):
        return (carry[0] >= 0) & (carry[1] > 0)

    def body(carry):
        j = carry[0]
        start = pl.multiple_of(j * c, c)
        allowed = (rel < (qi - j) * c) & (lane2 >= pad - j * c)
        rng = range(pairs)
        zs = [_dot_nt(q_cat[p], k_ref[pl.ds(start, c), p * c:(p + 1) * c]) for p in rng]
        log_1m, log_b, hi, lo = [], [], [], []
        for p in rng:
            t = jnp.log(1.0 + jnp.exp(-jnp.abs(zs[p])))
            l1 = jnp.where(allowed, -(jnp.maximum(zs[p], 0.0) + t), 0.0)
            log_1m.append(l1)
            log_b.append(jnp.minimum(zs[p], 0.0) - t)
            hi.append(l1.astype(BF16))
            lo.append((l1 - hi[p].astype(F32)).astype(BF16))
        local = [_dot(hi[p], upper_strict) + _dot(lo[p], upper_strict) for p in rng]
        w_cat = []
        for p in rng:
            wgt = jnp.where(allowed, jnp.exp(log_b[p] + local[p] + carry[3 + 2 * p]), 0.0).astype(BF16)
            w_cat.append(jnp.concatenate([wgt[:c], wgt[c:]], axis=1))
        out = []
        worst = None
        for p in rng:
            vj = v_ref[pl.ds(start, c), p * c:(p + 1) * c]
            acc = carry[2 + 2 * p] + _dot(w_cat[p], split_heads(vj))
            run = carry[3 + 2 * p] + jnp.sum(log_1m[p], axis=-1, keepdims=True)
            out += [acc, run]
            worst = run if worst is None else jnp.maximum(worst, run)
        live = (jnp.max(worst) >= -SB_UNDERFLOW).astype(jnp.int32)
        return (j - 1, live, *out)

    init = [qi, jnp.int32(1)]
    for p in range(pairs):
        init += [jnp.zeros((c, c), F32), jnp.zeros((2 * c, 1), F32)]
    res = lax.while_loop(cond, body, tuple(init))
    for p in range(pairs):
        o_ref[:, p * c:(p + 1) * c] = res[2 + 2 * p].astype(o_ref.dtype)


def _stick_breaking(sb_in, pad):
    b, t, _ = sb_in.shape
    nb = t // BLK
    pairs = SB_HEADS * SB_DH // BLK
    w = pairs * BLK
    return pl.pallas_call(
        functools.partial(_sb_kernel, pad=pad, pairs=pairs),
        grid=(b, nb),
        in_specs=[
            pl.BlockSpec((None, BLK, w), lambda i, j: (i, j, 0)),
            pl.BlockSpec((None, t, w), lambda i, j: (i, 0, 1)),
            pl.BlockSpec((None, t, w), lambda i, j: (i, 0, 2)),
        ],
        out_specs=pl.BlockSpec((None, BLK, w), lambda i, j: (i, j, 0)),
        out_shape=jax.ShapeDtypeStruct((b, t, w), BF16),
        compiler_params=_cparams(("parallel", "arbitrary")),
        name="stick_breaking",
    )(sb_in, sb_in, sb_in)


def _row_in_group(x, idx, group):
    c, d = x.shape
    x3 = x.reshape(c // group, group, d)
    return jnp.broadcast_to(x3[:, idx:idx + 1, :], x3.shape).reshape(c, d)


def _hgrn_intra(q, k, b):
    c = q.shape[0]
    row = _iota2((c, c), 0)
    col = _iota2((c, c), 1)
    a = jnp.zeros((c, c), F32)
    half = c // 2
    while half >= 8:
        ref_row = _row_in_group(b, half, 2 * half)
        lower = (row % (2 * half)) >= half
        e = jnp.exp(jnp.where(lower, b - ref_row, ref_row - b))
        ql = jnp.where(lower, q * e, 0.0).astype(BF16)
        kl = jnp.where(lower, 0.0, k * e).astype(BF16)
        same = (row // (2 * half)) == (col // (2 * half))
        a = a + jnp.where(same, _dot_nt(ql, kl), 0.0)
        half //= 2
    grp = row // 8 * 8
    for s in range(8):
        ks = _row_in_group(k, s, 8)
        bs = _row_in_group(b, s, 8)
        p = q * ks * jnp.exp(jnp.minimum(b - bs, 0.0))
        tot = jnp.sum(p, axis=-1, keepdims=True)
        hit = (col == grp + s) & (row % 8 >= s)
        a = jnp.where(hit, tot, a)
    return a


def _hgrn_kernel(q_ref, f_ref, i_ref, z_ref, lbraw_ref, gn_ref, o_ref, s_ref, *, pad, layer, heads):
    c = BLK
    j = pl.program_id(2)

    @pl.when(j == 0)
    def _():
        s_ref[...] = jnp.zeros_like(s_ref)

    row = _iota2((c, c), 0)
    col = _iota2((c, c), 1)
    tril = (row >= col).astype(F32)
    real = (j * c + row) >= pad
    for h in range(heads):
        sl = slice(h * HG_D, (h + 1) * HG_D)
        raw = lbraw_ref[:, sl]
        ex = jnp.exp(raw - jnp.max(raw, axis=0, keepdims=True))
        sm = ex / jnp.sum(ex, axis=0, keepdims=True)
        lb = jnp.sum(sm[1:layer + 1, :], axis=0, keepdims=True)
        q = _silu(q_ref[:, sl])
        fgate = lb + (1.0 - lb) * _sigmoid(f_ref[:, sl])
        logf = jnp.where(real, jnp.log(fgate), 0.0)
        k = jnp.where(real, 1.0 - fgate, 0.0)
        v = jnp.where(real, i_ref[:, sl], 0.0)
        vb = v.astype(BF16)
        b = _dot_exact(tril, logf)
        b_last = b[c - 1:c, :]
        s_t = s_ref[h]
        o = _dot_nt((q * jnp.exp(b)).astype(BF16), s_t.astype(BF16))
        a = _hgrn_intra(q, k, b)
        o = o + _dot(a.astype(BF16), vb)
        k_dec = (k * jnp.exp(b_last - b)).astype(BF16)
        s_ref[h] = s_t * jnp.exp(b_last) + _dot_tn(vb, k_dec)
        o = o * lax.rsqrt(jnp.mean(o * o, axis=-1, keepdims=True) + RMS_EPS) * gn_ref[...]
        o_ref[:, sl] = (o * _silu(z_ref[:, sl])).astype(o_ref.dtype)


def _hgrn2(c_in, lb_raw, gn_row, pad, layer, heads_per_step=2):
    b, t, _ = c_in.shape
    nc = t // BLK
    groups = HG_HEADS // heads_per_step
    w = heads_per_step * HG_D
    depth = lb_raw.shape[0]
    col_spec = lambda part: pl.BlockSpec((None, BLK, w), lambda i, g, j: (i, j, part * groups + g))
    return pl.pallas_call(
        functools.partial(_hgrn_kernel, pad=pad, layer=layer, heads=heads_per_step),
        grid=(b, groups, nc),
        in_specs=[
            col_spec(0), col_spec(1), col_spec(2), col_spec(3),
            pl.BlockSpec((depth, w), lambda i, g, j: (0, g)),
            pl.BlockSpec(gn_row.shape, lambda i, g, j: (0, 0)),
        ],
        out_specs=pl.BlockSpec((None, BLK, w), lambda i, g, j: (i, j, g)),
        out_shape=jax.ShapeDtypeStruct((b, t, HG_HEADS * HG_D), BF16),
        scratch_shapes=[pltpu.VMEM((heads_per_step, HG_D, HG_D), F32)],
        compiler_params=_cparams(("parallel", "parallel", "arbitrary")),
        name="hgrn2",
    )(c_in, c_in, c_in, c_in, lb_raw, gn_row)


def _lane_row(vec, offset, width=BLK):
    return jnp.zeros((1, width), F32).at[0, offset:offset + vec.shape[0]].set(vec.astype(F32))


def kernel(x, meta_tokens, ab_w_in, ab_conv_w, ab_a_log, ab_dt_bias, ab_gnorm_g, ab_w_out, c_w_in, c_lb_raw,
           c_gnorm_g, c_w_out, ln_mix_g, ln_mix_b, mlp_w1, mlp_w2, ln_ffn_g, ln_ffn_b):
    bsz, seq, d = x.shape
    n_meta = meta_tokens.shape[0]
    depth = ln_mix_g.shape[0]
    alpha = float((2 * depth) ** 0.25)
    pad = (-n_meta) % BLK
    t = pad + n_meta + seq
    assert seq % BLK == 0 and depth == 2
    m = bsz * t
    tm = 512
    assert m % tm == 0

    meta = jnp.broadcast_to(meta_tokens[None].astype(x.dtype), (bsz, n_meta, d))
    h = jnp.concatenate([jnp.zeros((bsz, pad, d), x.dtype), meta, x], axis=1).reshape(m, d)

    w_gdn = GDN_HEADS * GDN_D * 4
    w_sb = SB_HEADS * SB_DH * 3
    row2 = lambda a: a.astype(F32).reshape(1, -1)

    w_in = ab_w_in[0]
    wa = w_in[:, :w_gdn].astype(BF16)
    wg = jnp.pad(w_in[:, w_gdn:w_gdn + 2 * GDN_HEADS], ((0, 0), (0, BLK - 2 * GDN_HEADS))).astype(BF16)
    wb = w_in[:, w_gdn + 2 * GDN_HEADS:].astype(BF16)
    gdn_in, gate, sb_in = _project(h, [wa, wg, wb], [F32, F32, BF16], tm)
    oa = _gdn(gdn_in.reshape(bsz, t, w_gdn), gate.reshape(bsz, t, BLK), ab_conv_w[0].astype(F32),
              _lane_row(ab_a_log[0], GDN_HEADS), _lane_row(ab_dt_bias[0], GDN_HEADS), row2(ab_gnorm_g[0]))
    ob = _stick_breaking(sb_in.reshape(bsz, t, w_sb), pad)
    w_out = ab_w_out[0].astype(BF16)
    n_a = GDN_HEADS * GDN_D
    h = _post_mixer([oa.reshape(m, n_a), ob.reshape(m, -1)], [w_out[:n_a], w_out[n_a:]], h,
                    row2(ln_mix_g[0]), row2(ln_mix_b[0]), mlp_w1[0].astype(BF16), mlp_w2[0].astype(BF16),
                    row2(ln_ffn_g[0]), row2(ln_ffn_b[0]), alpha, tm)

    (c_in,) = _project(h, [c_w_in[0].astype(BF16)], [F32], tm)
    oc = _hgrn2(c_in.reshape(bsz, t, -1), c_lb_raw.astype(F32), row2(c_gnorm_g[0]), pad, 1)
    h = _post_mixer([oc.reshape(m, -1)], [c_w_out[0].astype(BF16)], h,
                    row2(ln_mix_g[1]), row2(ln_mix_b[1]), mlp_w1[1].astype(BF16), mlp_w2[1].astype(BF16),
                    row2(ln_ffn_g[1]), row2(ln_ffn_b[1]), alpha, tm)
    return h.reshape(bsz, t, d)[:, pad + n_meta:]
```

```python
import functools
import math

import jax
import jax.numpy as jnp
from jax import lax
from jax.experimental import pallas as pl
from jax.experimental.pallas import tpu as pltpu

F32 = jnp.float32
BF16 = jnp.bfloat16

BLK = 128
LN_EPS = 1e-5
RMS_EPS = 1e-6
L2_EPS = 1e-6
GDN_HEADS = 4
GDN_D = 128
GDN_CONV = 4
SB_HEADS = 8
SB_DH = 64
HG_HEADS = 8
HG_D = 128
SB_UNDERFLOW_LOG2 = 151.0
LOG2E = 1.4426950408889634
VMEM_LIMIT = 56 * 1024 * 1024


def _cparams(sem):
    return pltpu.CompilerParams(dimension_semantics=sem, vmem_limit_bytes=VMEM_LIMIT)


def _dot(a, b):
    return jnp.dot(a, b, preferred_element_type=F32)


def _dot_nt(a, b):
    return lax.dot_general(a, b, (((1,), (1,)), ((), ())), preferred_element_type=F32)


def _dot_tn(a, b):
    return lax.dot_general(a, b, (((0,), (0,)), ((), ())), preferred_element_type=F32)


def _sigmoid(x):
    return 1.0 / (1.0 + jnp.exp(-x))


def _silu(x):
    return x * _sigmoid(x)


def _softplus(x):
    return jnp.maximum(x, 0.0) + jnp.log(1.0 + jnp.exp(-jnp.abs(x)))


def _iota2(shape, dim):
    return lax.broadcasted_iota(jnp.int32, shape, dim)


def _layer_norm(x, g, b):
    mu = jnp.mean(x, axis=-1, keepdims=True)
    xc = x - mu
    var = jnp.mean(xc * xc, axis=-1, keepdims=True)
    return xc * lax.rsqrt(var + LN_EPS) * g + b


def _proj_kernel(x_ref, *refs, n_out):
    w_refs, o_refs = refs[:n_out], refs[n_out:]
    xb = x_ref[...].astype(BF16)
    for w_ref, o_ref in zip(w_refs, o_refs):
        o_ref[...] = _dot(xb, w_ref[...]).astype(o_ref.dtype)


def _project(x2d, weights, out_dtypes, tm):
    m, d = x2d.shape
    n_out = len(weights)
    in_specs = [pl.BlockSpec((tm, d), lambda i: (i, 0))]
    for w in weights:
        in_specs.append(pl.BlockSpec(w.shape, lambda i: (0, 0), pipeline_mode=pl.Buffered(1)))
    out_specs = [pl.BlockSpec((tm, w.shape[1]), lambda i: (i, 0)) for w in weights]
    out_shape = [jax.ShapeDtypeStruct((m, w.shape[1]), dt) for w, dt in zip(weights, out_dtypes)]
    return pl.pallas_call(
        functools.partial(_proj_kernel, n_out=n_out),
        grid=(m // tm,),
        in_specs=in_specs,
        out_specs=out_specs,
        out_shape=out_shape,
        compiler_params=_cparams(("parallel",)),
        name="proj",
    )(x2d, *weights)


def _post_kernel(*refs, n_mix, alpha, ff_chunks):
    y_refs = refs[:n_mix]
    wo_refs = refs[n_mix:2 * n_mix]
    h_ref, g1_ref, b1_ref, w1_ref, w2_ref, g2_ref, b2_ref, o_ref = refs[2 * n_mix:]
    mix = _dot(y_refs[0][...], wo_refs[0][...])
    for y_ref, wo_ref in zip(y_refs[1:], wo_refs[1:]):
        mix = mix + _dot(y_ref[...], wo_ref[...])
    h1 = _layer_norm(alpha * h_ref[...] + mix, g1_ref[...], b1_ref[...])
    h1b = h1.astype(BF16)
    d_ff = w1_ref.shape[1]
    fc = d_ff // ff_chunks
    acc = None
    for j in range(ff_chunks):
        hid = _dot(h1b, w1_ref[:, j * fc:(j + 1) * fc])
        hid = jnp.square(jnp.maximum(hid, 0.0)).astype(BF16)
        part = _dot(hid, w2_ref[j * fc:(j + 1) * fc, :])
        acc = part if acc is None else acc + part
    o_ref[...] = _layer_norm(alpha * h1 + acc, g2_ref[...], b2_ref[...])


def _post_mixer(ys, wos, h2d, g1, b1, w1, w2, g2, b2, alpha, tm):
    m, d = h2d.shape
    n_mix = len(ys)
    row = lambda i: (i, 0)
    fixed = lambda i: (0, 0)
    const = lambda a: pl.BlockSpec(a.shape, fixed, pipeline_mode=pl.Buffered(1))
    in_specs = [pl.BlockSpec((tm, y.shape[1]), row) for y in ys]
    in_specs += [const(w) for w in wos]
    in_specs += [pl.BlockSpec((tm, d), row), const(g1), const(b1), const(w1), const(w2), const(g2), const(b2)]
    return pl.pallas_call(
        functools.partial(_post_kernel, n_mix=n_mix, alpha=alpha, ff_chunks=4),
        grid=(m // tm,),
        in_specs=in_specs,
        out_specs=pl.BlockSpec((tm, d), row),
        out_shape=jax.ShapeDtypeStruct((m, d), F32),
        compiler_params=_cparams(("parallel",)),
        name="post_mixer",
    )(*ys, *wos, h2d, g1, b1, w1, w2, g2, b2)


def _split3(x):
    hi = x.astype(BF16)
    r1 = x - hi.astype(F32)
    mid = r1.astype(BF16)
    lo = (r1 - mid.astype(F32)).astype(BF16)
    return hi, mid, lo


def _cumsum_rows(x):
    c = x.shape[0]
    tril = (_iota2((c, c), 0) >= _iota2((c, c), 1)).astype(BF16)
    return _dot(jnp.concatenate([tril, tril, tril], axis=1), jnp.concatenate(_split3(x), axis=0))


def _unit_lower_inverses(ms):
    c = ms[0].shape[0]
    assert c == 128
    eye = (_iota2((c, c), 0) == _iota2((c, c), 1)).astype(F32)
    bf = lambda xs: [x.astype(BF16) for x in xs]
    sq = lambda xs: [_dot(x, x) for x in xs]
    mul = lambda xs, ys: [_dot(x, y) for x, y in zip(xs, ys)]
    x1 = [-m for m in ms]
    x1b = bf(x1)
    x2 = sq(x1b)
    x2b = bf(x2)
    x4 = sq(x2b)
    x12 = mul(x1b, x2b)
    x4b = bf(x4)
    x8 = sq(x4b)
    g01 = [eye + a + b + ab for a, b, ab in zip(x1, x2, x12)]
    x8b = bf(x8)
    x16 = sq(x8b)
    x48 = mul(x4b, x8b)
    x16b = bf(x16)
    g23 = [eye + a + b + ab for a, b, ab in zip(x4, x8, x48)]
    x32 = sq(x16b)
    g0123 = mul(bf(g01), bf(g23))
    x32b = bf(x32)
    x64 = sq(x32b)
    x1632 = mul(x16b, x32b)
    g45 = [eye + a + b + ab for a, b, ab in zip(x16, x32, x1632)]
    g456 = [g + gx for g, gx in zip(g45, mul(bf(g45), bf(x64)))]
    return mul(bf(g0123), bf(g456))


def _gdn_kernel(qkv_ref, z_ref, gate_ref, convw_ref, alog_ref, dtb_ref, gn_ref, o_ref, xc_ref, s_ref):
    c = BLK
    nh = GDN_HEADS
    d = GDN_D
    w_qkv = nh * d * 3
    hs = range(nh)

    @pl.when(pl.program_id(1) == 0)
    def _():
        xc_ref[0:8, :] = jnp.zeros((8, w_qkv), F32)
        s_ref[...] = jnp.zeros_like(s_ref)

    xc_ref[8:8 + c, :] = qkv_ref[...]
    conv = None
    for k in range(GDN_CONV):
        off = 8 - (GDN_CONV - 1) + k
        term = xc_ref[off:off + c, :] * convw_ref[k:k + 1, :]
        conv = term if conv is None else conv + term
    xc_ref[0:8, :] = xc_ref[c:c + 8, :]
    qkv = _silu(conv)

    gate = gate_ref[...]
    beta_all = _sigmoid(gate)
    g_all = -jnp.exp(alog_ref[...]) * _softplus(gate + dtb_ref[...])
    gc_all = _cumsum_rows(g_all)
    gc_all_t = gc_all.T
    row = _iota2((c, c), 0)
    col = _iota2((c, c), 1)
    causal = row >= col
    strict = row > col

    def l2n(x):
        return x * lax.rsqrt(jnp.sum(x * x, axis=-1, keepdims=True) + L2_EPS)

    q = [l2n(qkv[:, h * d:(h + 1) * d]) * (d ** -0.5) for h in hs]
    k = [l2n(qkv[:, (nh + h) * d:(nh + h + 1) * d]) for h in hs]
    v = [qkv[:, (2 * nh + h) * d:(2 * nh + h + 1) * d] for h in hs]
    beta = [beta_all[:, h:h + 1] for h in hs]
    gc = [gc_all[:, nh + h:nh + h + 1] for h in hs]
    gc_last = [gc_all[c - 1:c, nh + h:nh + h + 1] for h in hs]
    decay = [jnp.where(causal, jnp.exp(jnp.minimum(gc[h] - gc_all_t[nh + h:nh + h + 1, :], 0.0)), 0.0) for h in hs]
    kb = [k[h] * beta[h] for h in hs]
    kbf = [k[h].astype(BF16) for h in hs]
    kk = [_dot_nt(kb[h].astype(BF16), kbf[h]) for h in hs]
    qk = [_dot_nt(q[h].astype(BF16), kbf[h]) for h in hs]
    t_inv = _unit_lower_inverses([jnp.where(strict, kk[h] * decay[h], 0.0) for h in hs])
    t_invb = [t.astype(BF16) for t in t_inv]
    egc = [jnp.exp(gc[h]) for h in hs]
    u = [_dot(t_invb[h], (v[h] * beta[h]).astype(BF16)) for h in hs]
    w = [_dot(t_invb[h], (kb[h] * egc[h]).astype(BF16)).astype(BF16) for h in hs]
    a_intra = [(qk[h] * decay[h]).astype(BF16) for h in hs]
    q_dec = [(q[h] * egc[h]).astype(BF16) for h in hs]
    k_dec = [(k[h] * jnp.exp(gc_last[h] - gc[h])).astype(BF16) for h in hs]
    s = [s_ref[h] for h in hs]
    sb = [s[h].astype(BF16) for h in hs]
    v_new = [(u[h] - _dot(w[h], sb[h])).astype(BF16) for h in hs]
    o_state = [_dot(q_dec[h], sb[h]) for h in hs]
    o_intra = [_dot(a_intra[h], v_new[h]) for h in hs]
    s_add = [_dot_tn(k_dec[h], v_new[h]) for h in hs]
    for h in hs:
        s_ref[h] = s[h] * jnp.exp(gc_last[h]) + s_add[h]
        o = o_state[h] + o_intra[h]
        o = o * lax.rsqrt(jnp.mean(o * o, axis=-1, keepdims=True) + RMS_EPS) * gn_ref[...]
        o_ref[:, h * d:(h + 1) * d] = (o * _silu(z_ref[:, h * d:(h + 1) * d])).astype(o_ref.dtype)


def _gdn(gdn_in, gate, conv_w, alog_row, dtb_row, gn_row):
    b, t, _ = gdn_in.shape
    nc = t // BLK
    w_qkv = GDN_HEADS * GDN_D * 3
    w_v = GDN_HEADS * GDN_D
    fixed = lambda i, j: (0, 0)
    return pl.pallas_call(
        _gdn_kernel,
        grid=(b, nc),
        in_specs=[
            pl.BlockSpec((None, BLK, w_qkv), lambda i, j: (i, j, 0)),
            pl.BlockSpec((None, BLK, w_v), lambda i, j: (i, j, 3)),
            pl.BlockSpec((None, BLK, BLK), lambda i, j: (i, j, 0)),
            pl.BlockSpec(conv_w.shape, fixed),
            pl.BlockSpec(alog_row.shape, fixed),
            pl.BlockSpec(dtb_row.shape, fixed),
            pl.BlockSpec(gn_row.shape, fixed),
        ],
        out_specs=pl.BlockSpec((None, BLK, w_v), lambda i, j: (i, j, 0)),
        out_shape=jax.ShapeDtypeStruct((b, t, w_v), BF16),
        scratch_shapes=[
            pltpu.VMEM((BLK + 8, w_qkv), F32),
            pltpu.VMEM((GDN_HEADS, GDN_D, GDN_D), F32),
        ],
        compiler_params=_cparams(("parallel", "arbitrary")),
        name="gdn",
    )(gdn_in, gdn_in, gate, conv_w, alog_row, dtb_row, gn_row)


def _sb_kernel(q_ref, k_ref, v_ref, o_ref, *, pad, pairs):
    c = BLK
    qi = pl.program_id(1)
    lane2 = _iota2((2 * c, c), 1)
    rel = lane2 - (_iota2((2 * c, c), 0) & (c - 1))
    upper_strict = (_iota2((c, c), 0) > _iota2((c, c), 1)).astype(BF16)
    upper_strict2 = jnp.concatenate([upper_strict, upper_strict], axis=0)
    head0 = _iota2((c, c), 1) < SB_DH
    scale = jnp.asarray(SB_DH ** -0.5, BF16)
    zero_b = jnp.zeros((c, c), BF16)

    def split_heads(x):
        return jnp.concatenate([jnp.where(head0, x, zero_b), jnp.where(head0, zero_b, x)], axis=0)

    q_cat = [split_heads(q_ref[:, p * c:(p + 1) * c] * scale) for p in range(pairs)]

    def cond(carry):
        return (carry[0] >= 0) & (carry[1] > 0)

    def body(carry):
        j = carry[0]
        start = pl.multiple_of(j * c, c)
        allowed = (rel < (qi - j) * c) & (lane2 >= pad - j * c)
        rng = range(pairs)
        zs = [_dot_nt(q_cat[p], k_ref[pl.ds(start, c), p * c:(p + 1) * c]) * LOG2E for p in rng]
        log_1m, log_b, hilo = [], [], []
        for p in rng:
            t = jnp.log2(1.0 + jnp.exp2(-jnp.abs(zs[p])))
            l1 = jnp.where(allowed, -(jnp.maximum(zs[p], 0.0) + t), 0.0)
            log_1m.append(l1)
            log_b.append(jnp.minimum(zs[p], 0.0) - t)
            hi = l1.astype(BF16)
            hilo.append(jnp.concatenate([hi, (l1 - hi.astype(F32)).astype(BF16)], axis=1))
        local = [_dot(hilo[p], upper_strict2) for p in rng]
        w_cat = []
        for p in rng:
            wgt = jnp.where(allowed, jnp.exp2(log_b[p] + local[p] + carry[3 + 2 * p]), 0.0).astype(BF16)
            w_cat.append(jnp.concatenate([wgt[:c], wgt[c:]], axis=1))
        out = []
        worst = None
        for p in rng:
            vj = v_ref[pl.ds(start, c), p * c:(p + 1) * c]
            acc = carry[2 + 2 * p] + _dot(w_cat[p], split_heads(vj))
            run = carry[3 + 2 * p] + jnp.sum(log_1m[p], axis=-1, keepdims=True)
            out += [acc, run]
            worst = run if worst is None else jnp.maximum(worst, run)
        live = (jnp.max(worst) >= -SB_UNDERFLOW_LOG2).astype(jnp.int32)
        return (j - 1, live, *out)

    init = [qi, jnp.int32(1)]
    for p in range(pairs):
        init += [jnp.zeros((c, c), F32), jnp.zeros((2 * c, 1), F32)]
    res = lax.while_loop(cond, body, tuple(init))
    for p in range(pairs):
        o_ref[:, p * c:(p + 1) * c] = res[2 + 2 * p].astype(o_ref.dtype)


def _stick_breaking(sb_in, pad):
    b, t, _ = sb_in.shape
    nb = t // BLK
    pairs = SB_HEADS * SB_DH // BLK
    w = pairs * BLK
    return pl.pallas_call(
        functools.partial(_sb_kernel, pad=pad, pairs=pairs),
        grid=(b, nb),
        in_specs=[
            pl.BlockSpec((None, BLK, w), lambda i, j: (i, j, 0)),
            pl.BlockSpec((None, t, w), lambda i, j: (i, 0, 1)),
            pl.BlockSpec((None, t, w), lambda i, j: (i, 0, 2)),
        ],
        out_specs=pl.BlockSpec((None, BLK, w), lambda i, j: (i, j, 0)),
        out_shape=jax.ShapeDtypeStruct((b, t, w), BF16),
        compiler_params=_cparams(("parallel", "arbitrary")),
        name="stick_breaking",
    )(sb_in, sb_in, sb_in)


def _row_in_group(x, idx, group):
    c, d = x.shape
    x3 = x.reshape(c // group, group, d)
    return jnp.broadcast_to(x3[:, idx:idx + 1, :], x3.shape).reshape(c, d)


def _split_row(b, half):
    if half >= 8:
        return _row_in_group(b, half, 2 * half)
    sub = _iota2(b.shape, 0) & 7
    picks = [_row_in_group(b, s, 8) for s in range(half, 8, 2 * half)]
    out = picks[-1]
    for i in reversed(range(len(picks) - 1)):
        out = jnp.where(sub < (i + 1) * 2 * half, picks[i], out)
    return out


def _hgrn_intra(qs, ks, bs):
    c = qs[0].shape[0]
    hs = range(len(qs))
    row = _iota2((c, c), 0)
    col = _iota2((c, c), 1)
    levels = []
    half = c // 2
    while half >= 1:
        lower = (row & (2 * half - 1)) >= half
        prods = []
        for h in hs:
            e = jnp.exp(-jnp.abs(bs[h] - _split_row(bs[h], half)))
            ql = jnp.where(lower, qs[h] * e, 0.0).astype(BF16)
            kl = jnp.where(lower, 0.0, ks[h] * e).astype(BF16)
            prods.append(_dot_nt(ql, kl))
        levels.append((half, prods))
        half //= 2
    diag = [jnp.sum(qs[h] * ks[h], axis=-1, keepdims=True) for h in hs]
    out = []
    for h in hs:
        a = levels[0][1][h]
        for half, prods in levels[1:]:
            shift = int(math.log2(2 * half))
            a = jnp.where((row >> shift) == (col >> shift), prods[h], a)
        out.append(jnp.where(row == col, diag[h], a))
    return out


def _hgrn_kernel(q_ref, f_ref, i_ref, z_ref, lbraw_ref, gn_ref, o_ref, s_ref, *, pad, layer, heads):
    c = BLK
    d = HG_D
    j = pl.program_id(2)
    hs = range(heads)

    @pl.when(j == 0)
    def _():
        s_ref[...] = jnp.zeros_like(s_ref)

    real = (j * c + _iota2((c, d), 0)) >= pad
    raw = lbraw_ref[...]
    ex = jnp.exp(raw - jnp.max(raw, axis=0, keepdims=True))
    sm = ex / jnp.sum(ex, axis=0, keepdims=True)
    lb_all = jnp.sum(sm[1:layer + 1, :], axis=0, keepdims=True)

    sl = [slice(h * d, (h + 1) * d) for h in hs]
    lb = [lb_all[:, sl[h]] for h in hs]
    q = [_silu(q_ref[:, sl[h]]) for h in hs]
    fgate = [lb[h] + (1.0 - lb[h]) * _sigmoid(f_ref[:, sl[h]]) for h in hs]
    k = [1.0 - fgate[h] for h in hs]
    vb = [jnp.where(real, i_ref[:, sl[h]], 0.0).astype(BF16) for h in hs]
    b = [_cumsum_rows(jnp.log(fgate[h])) for h in hs]
    b_last = [b[h][c - 1:c, :] for h in hs]
    s_t = [s_ref[h] for h in hs]
    o_state = [_dot_nt((q[h] * jnp.exp(b[h])).astype(BF16), s_t[h].astype(BF16)) for h in hs]
    a = _hgrn_intra(q, k, b)
    o_intra = [_dot(a[h].astype(BF16), vb[h]) for h in hs]
    k_dec = [(k[h] * jnp.exp(b_last[h] - b[h])).astype(BF16) for h in hs]
    s_add = [_dot_tn(vb[h], k_dec[h]) for h in hs]
    for h in hs:
        s_ref[h] = s_t[h] * jnp.exp(b_last[h]) + s_add[h]
        o = o_state[h] + o_intra[h]
        o = o * lax.rsqrt(jnp.mean(o * o, axis=-1, keepdims=True) + RMS_EPS) * gn_ref[...]
        o_ref[:, sl[h]] = (o * _silu(z_ref[:, sl[h]])).astype(o_ref.dtype)


def _hgrn2(c_in, lb_raw, gn_row, pad, layer, heads_per_step=4):
    b, t, _ = c_in.shape
    nc = t // BLK
    groups = HG_HEADS // heads_per_step
    w = heads_per_step * HG_D
    depth = lb_raw.shape[0]
    col_spec = lambda part: pl.BlockSpec((None, BLK, w), lambda i, g, j: (i, j, part * groups + g))
    return pl.pallas_call(
        functools.partial(_hgrn_kernel, pad=pad, layer=layer, heads=heads_per_step),
        grid=(b, groups, nc),
        in_specs=[
            col_spec(0), col_spec(1), col_spec(2), col_spec(3),
            pl.BlockSpec((depth, w), lambda i, g, j: (0, g)),
            pl.BlockSpec(gn_row.shape, lambda i, g, j: (0, 0)),
        ],
        out_specs=pl.BlockSpec((None, BLK, w), lambda i, g, j: (i, j, g)),
        out_shape=jax.ShapeDtypeStruct((b, t, HG_HEADS * HG_D), BF16),
        scratch_shapes=[pltpu.VMEM((heads_per_step, HG_D, HG_D), F32)],
        compiler_params=_cparams(("parallel", "parallel", "arbitrary")),
        name="hgrn2",
    )(c_in, c_in, c_in, c_in, lb_raw, gn_row)


def _lane_row(vec, offset, width=BLK):
    return jnp.zeros((1, width), F32).at[0, offset:offset + vec.shape[0]].set(vec.astype(F32))


def kernel(x, meta_tokens, ab_w_in, ab_conv_w, ab_a_log, ab_dt_bias, ab_gnorm_g, ab_w_out, c_w_in, c_lb_raw,
           c_gnorm_g, c_w_out, ln_mix_g, ln_mix_b, mlp_w1, mlp_w2, ln_ffn_g, ln_ffn_b):
    bsz, seq, d = x.shape
    n_meta = meta_tokens.shape[0]
    depth = ln_mix_g.shape[0]
    alpha = float((2 * depth) ** 0.25)
    pad = (-n_meta) % BLK
    t = pad + n_meta + seq
    assert seq % BLK == 0 and depth == 2
    m = bsz * t
    tm = 512
    assert m % tm == 0

    meta = jnp.broadcast_to(meta_tokens[None].astype(x.dtype), (bsz, n_meta, d))
    h = jnp.concatenate([jnp.zeros((bsz, pad, d), x.dtype), meta, x], axis=1).reshape(m, d)

    w_gdn = GDN_HEADS * GDN_D * 4
    w_sb = SB_HEADS * SB_DH * 3
    row2 = lambda a: a.astype(F32).reshape(1, -1)

    w_in = ab_w_in[0]
    wa = w_in[:, :w_gdn].astype(BF16)
    wg = jnp.pad(w_in[:, w_gdn:w_gdn + 2 * GDN_HEADS], ((0, 0), (0, BLK - 2 * GDN_HEADS))).astype(BF16)
    wb = w_in[:, w_gdn + 2 * GDN_HEADS:].astype(BF16)
    gdn_in, gate, sb_in = _project(h, [wa, wg, wb], [F32, F32, BF16], tm)
    oa = _gdn(gdn_in.reshape(bsz, t, w_gdn), gate.reshape(bsz, t, BLK), ab_conv_w[0].astype(F32),
              _lane_row(ab_a_log[0], GDN_HEADS), _lane_row(ab_dt_bias[0], GDN_HEADS), row2(ab_gnorm_g[0]))
    ob = _stick_breaking(sb_in.reshape(bsz, t, w_sb), pad)
    w_out = ab_w_out[0].astype(BF16)
    n_a = GDN_HEADS * GDN_D
    h = _post_mixer([oa.reshape(m, n_a), ob.reshape(m, -1)], [w_out[:n_a], w_out[n_a:]], h,
                    row2(ln_mix_g[0]), row2(ln_mix_b[0]), mlp_w1[0].astype(BF16), mlp_w2[0].astype(BF16),
                    row2(ln_ffn_g[0]), row2(ln_ffn_b[0]), alpha, tm)

    (c_in,) = _project(h, [c_w_in[0].astype(BF16)], [F32], tm)
    oc = _hgrn2(c_in.reshape(bsz, t, -1), c_lb_raw.astype(F32), row2(c_gnorm_g[0]), pad, 1)
    h = _post_mixer([oc.reshape(m, -1)], [c_w_out[0].astype(BF16)], h,
                    row2(ln_mix_g[1]), row2(ln_mix_b[1]), mlp_w1[1].astype(BF16), mlp_w2[1].astype(BF16),
                    row2(ln_ffn_g[1]), row2(ln_ffn_b[1]), alpha, tm)
    return h.reshape(bsz, t, d)[:, pad + n_meta:]
```

```python
import functools
import math

import jax
import jax.numpy as jnp
from jax import lax
from jax.experimental import pallas as pl
from jax.experimental.pallas import tpu as pltpu

F32 = jnp.float32
BF16 = jnp.bfloat16

BLK = 128
LN_EPS = 1e-5
RMS_EPS = 1e-6
L2_EPS = 1e-6
GDN_HEADS = 4
GDN_D = 128
GDN_CONV = 4
GDN_BATCH_PER_STEP = 4
SB_HEADS = 8
SB_DH = 64
HG_HEADS = 8
HG_D = 128
SB_UNDERFLOW_LOG2 = 151.0
LOG2E = 1.4426950408889634
SB_KEY_BLOCKS = 3
VMEM_LIMIT = 56 * 1024 * 1024


def _cparams(sem):
    return pltpu.CompilerParams(dimension_semantics=sem, vmem_limit_bytes=VMEM_LIMIT)


def _dot(a, b):
    return jnp.dot(a, b, preferred_element_type=F32)


def _dot_nt(a, b):
    return lax.dot_general(a, b, (((1,), (1,)), ((), ())), preferred_element_type=F32)


def _dot_tn(a, b):
    return lax.dot_general(a, b, (((0,), (0,)), ((), ())), preferred_element_type=F32)


def _exp_neg(x):
    return jnp.exp2(x * (-LOG2E))


def _sigmoid(x):
    return 1.0 / (1.0 + _exp_neg(x))


def _silu(x):
    return x * _sigmoid(x)


def _softplus(x):
    return jnp.maximum(x, 0.0) + jnp.log(1.0 + jnp.exp(-jnp.abs(x)))


def _iota2(shape, dim):
    return lax.broadcasted_iota(jnp.int32, shape, dim)


def _layer_norm(x, g, b):
    mu = jnp.mean(x, axis=-1, keepdims=True)
    xc = x - mu
    var = jnp.mean(xc * xc, axis=-1, keepdims=True)
    return xc * lax.rsqrt(var + LN_EPS) * g + b


def _proj_kernel(x_ref, *refs, n_out):
    w_refs, o_refs = refs[:n_out], refs[n_out:]
    xb = x_ref[...].astype(BF16)
    for w_ref, o_ref in zip(w_refs, o_refs):
        o_ref[...] = _dot(xb, w_ref[...]).astype(o_ref.dtype)


def _project(x2d, weights, out_dtypes, tm):
    m, d = x2d.shape
    n_out = len(weights)
    in_specs = [pl.BlockSpec((tm, d), lambda i: (i, 0))]
    for w in weights:
        in_specs.append(pl.BlockSpec(w.shape, lambda i: (0, 0), pipeline_mode=pl.Buffered(1)))
    out_specs = [pl.BlockSpec((tm, w.shape[1]), lambda i: (i, 0)) for w in weights]
    out_shape = [jax.ShapeDtypeStruct((m, w.shape[1]), dt) for w, dt in zip(weights, out_dtypes)]
    return pl.pallas_call(
        functools.partial(_proj_kernel, n_out=n_out),
        grid=(m // tm,),
        in_specs=in_specs,
        out_specs=out_specs,
        out_shape=out_shape,
        compiler_params=_cparams(("parallel",)),
        name="proj",
    )(x2d, *weights)


def _post_kernel(*refs, n_mix, alpha, ff_chunks):
    y_refs = refs[:n_mix]
    wo_refs = refs[n_mix:2 * n_mix]
    h_ref, g1_ref, b1_ref, w1_ref, w2_ref, g2_ref, b2_ref, o_ref = refs[2 * n_mix:]
    mix = _dot(y_refs[0][...], wo_refs[0][...])
    for y_ref, wo_ref in zip(y_refs[1:], wo_refs[1:]):
        mix = mix + _dot(y_ref[...], wo_ref[...])
    h1 = _layer_norm(alpha * h_ref[...] + mix, g1_ref[...], b1_ref[...])
    h1b = h1.astype(BF16)
    d_ff = w1_ref.shape[1]
    fc = d_ff // ff_chunks
    acc = None
    for j in range(ff_chunks):
        hid = _dot(h1b, w1_ref[:, j * fc:(j + 1) * fc])
        hid = jnp.square(jnp.maximum(hid, 0.0)).astype(BF16)
        part = _dot(hid, w2_ref[j * fc:(j + 1) * fc, :])
        acc = part if acc is None else acc + part
    o_ref[...] = _layer_norm(alpha * h1 + acc, g2_ref[...], b2_ref[...])


def _post_mixer(ys, wos, h2d, g1, b1, w1, w2, g2, b2, alpha, tm, keep=None):
    m, d = h2d.shape
    n_mix = len(ys)
    const = lambda a: pl.BlockSpec(a.shape, lambda *_: (0, 0), pipeline_mode=pl.Buffered(1))
    if keep is None:
        grid = (m // tm,)
        row_in = lambda w: pl.BlockSpec((tm, w), lambda i: (i, 0))
        out_spec = pl.BlockSpec((tm, d), lambda i: (i, 0))
        m_out = m
    else:
        bsz, t, first, kept = keep
        assert kept % tm == 0 and first % 8 == 0 and t % 8 == 0
        grid = (bsz, kept // tm)
        row_in = lambda w: pl.BlockSpec((pl.Element(tm), pl.Element(w)), lambda b, i: (pl.multiple_of(b * t + first + i * tm, 8), 0))
        out_spec = pl.BlockSpec((tm, d), lambda b, i: (b * (kept // tm) + i, 0))
        m_out = bsz * kept
    in_specs = [row_in(y.shape[1]) for y in ys]
    in_specs += [const(w) for w in wos]
    in_specs += [row_in(d), const(g1), const(b1), const(w1), const(w2), const(g2), const(b2)]
    return pl.pallas_call(
        functools.partial(_post_kernel, n_mix=n_mix, alpha=alpha, ff_chunks=4),
        grid=grid,
        in_specs=in_specs,
        out_specs=out_spec,
        out_shape=jax.ShapeDtypeStruct((m_out, d), F32),
        compiler_params=_cparams(("parallel",) * len(grid)),
        name="post_mixer",
    )(*ys, *wos, h2d, g1, b1, w1, w2, g2, b2)


def _split3(x):
    hi = x.astype(BF16)
    r1 = x - hi.astype(F32)
    mid = r1.astype(BF16)
    lo = (r1 - mid.astype(F32)).astype(BF16)
    return hi, mid, lo


def _cumsum_rows(x):
    c = x.shape[0]
    tril = (_iota2((c, c), 0) >= _iota2((c, c), 1)).astype(BF16)
    return _dot(jnp.concatenate([tril, tril, tril], axis=1), jnp.concatenate(_split3(x), axis=0))


def _unit_lower_inverses(ms):
    c = ms[0].shape[0]
    assert c == 128
    eye = (_iota2((c, c), 0) == _iota2((c, c), 1)).astype(F32)
    bf = lambda xs: [x.astype(BF16) for x in xs]
    sq = lambda xs: [_dot(x, x) for x in xs]
    mul = lambda xs, ys: [_dot(x, y) for x, y in zip(xs, ys)]
    x1 = [-m for m in ms]
    x1b = bf(x1)
    x2 = sq(x1b)
    x2b = bf(x2)
    x4 = sq(x2b)
    x12 = mul(x1b, x2b)
    x4b = bf(x4)
    x8 = sq(x4b)
    g01 = [eye + a + b + ab for a, b, ab in zip(x1, x2, x12)]
    x8b = bf(x8)
    x16 = sq(x8b)
    x48 = mul(x4b, x8b)
    x16b = bf(x16)
    g23 = [eye + a + b + ab for a, b, ab in zip(x4, x8, x48)]
    x32 = sq(x16b)
    g0123 = mul(bf(g01), bf(g23))
    x32b = bf(x32)
    x64 = sq(x32b)
    x1632 = mul(x16b, x32b)
    g45 = [eye + a + b + ab for a, b, ab in zip(x16, x32, x1632)]
    g456 = [g + gx for g, gx in zip(g45, mul(bf(g45), bf(x64)))]
    return mul(bf(g0123), bf(g456))


def _gdn_kernel(qkv_ref, z_ref, gate_ref, convw_ref, alog_ref, dtb_ref, gn_ref, o_ref, xc_ref, s_ref, *, nb):
    c = BLK
    nh = GDN_HEADS
    d = GDN_D
    w_qkv = nh * d * 3
    chains = [(bb, h) for bb in range(nb) for h in range(nh)]
    ns = range(len(chains))

    @pl.when(pl.program_id(1) == 0)
    def _():
        xc_ref[...] = jnp.zeros_like(xc_ref)
        s_ref[...] = jnp.zeros_like(s_ref)

    row = _iota2((c, c), 0)
    col = _iota2((c, c), 1)
    causal = row >= col
    strict = row > col
    qkv, beta_all, gc_all, gc_all_t = [], [], [], []
    for bb in range(nb):
        x = qkv_ref[bb]
        tail = xc_ref[bb]
        sub8 = _iota2((8, w_qkv), 0)
        conv = x * convw_ref[GDN_CONV - 1:GDN_CONV, :]
        for k in range(1, GDN_CONV):
            rolled = pltpu.roll(x, k, 0)
            head = jnp.where(sub8 < k, pltpu.roll(tail, k, 0), rolled[0:8])
            shifted = jnp.concatenate([head, rolled[8:]], axis=0)
            conv = conv + shifted * convw_ref[GDN_CONV - 1 - k:GDN_CONV - k, :]
        xc_ref[bb] = x[c - 8:c]
        qkv.append(_silu(conv))
        gate = gate_ref[bb]
        beta_all.append(_sigmoid(gate))
        g_all = -jnp.exp(alog_ref[...]) * _softplus(gate + dtb_ref[...])
        gc_all.append(_cumsum_rows(g_all))
        gc_all_t.append(gc_all[bb].T)

    def l2n(x):
        return x * lax.rsqrt(jnp.sum(x * x, axis=-1, keepdims=True) + L2_EPS)

    q = [l2n(qkv[bb][:, h * d:(h + 1) * d]) * (d ** -0.5) for bb, h in chains]
    k = [l2n(qkv[bb][:, (nh + h) * d:(nh + h + 1) * d]) for bb, h in chains]
    v = [qkv[bb][:, (2 * nh + h) * d:(2 * nh + h + 1) * d] for bb, h in chains]
    beta = [beta_all[bb][:, h:h + 1] for bb, h in chains]
    gc = [gc_all[bb][:, nh + h:nh + h + 1] for bb, h in chains]
    gc_last = [gc_all[bb][c - 1:c, nh + h:nh + h + 1] for bb, h in chains]
    gc_row = [gc_all_t[bb][nh + h:nh + h + 1, :] for bb, h in chains]
    decay = [jnp.where(causal, jnp.exp(jnp.minimum(gc[n] - gc_row[n], 0.0)), 0.0) for n in ns]
    kb = [k[n] * beta[n] for n in ns]
    kbf = [k[n].astype(BF16) for n in ns]
    kk = [_dot_nt(kb[n].astype(BF16), kbf[n]) for n in ns]
    qk = [_dot_nt(q[n].astype(BF16), kbf[n]) for n in ns]
    t_inv = _unit_lower_inverses([jnp.where(strict, kk[n] * decay[n], 0.0) for n in ns])
    t_invb = [t.astype(BF16) for t in t_inv]
    egc = [jnp.exp(gc[n]) for n in ns]
    u = [_dot(t_invb[n], (v[n] * beta[n]).astype(BF16)) for n in ns]
    w = [_dot(t_invb[n], (kb[n] * egc[n]).astype(BF16)).astype(BF16) for n in ns]
    a_intra = [(qk[n] * decay[n]).astype(BF16) for n in ns]
    q_dec = [(q[n] * egc[n]).astype(BF16) for n in ns]
    k_dec = [(k[n] * jnp.exp(gc_last[n] - gc[n])).astype(BF16) for n in ns]
    s = [s_ref[bb, h] for bb, h in chains]
    sb = [s[n].astype(BF16) for n in ns]
    v_new = [(u[n] - _dot(w[n], sb[n])).astype(BF16) for n in ns]
    o_state = [_dot(q_dec[n], sb[n]) for n in ns]
    o_intra = [_dot(a_intra[n], v_new[n]) for n in ns]
    s_add = [_dot_tn(k_dec[n], v_new[n]) for n in ns]
    for n, (bb, h) in enumerate(chains):
        s_ref[bb, h] = s[n] * jnp.exp(gc_last[n]) + s_add[n]
        o = o_state[n] + o_intra[n]
        o = o * lax.rsqrt(jnp.mean(o * o, axis=-1, keepdims=True) + RMS_EPS) * gn_ref[...]
        o_ref[bb, :, h * d:(h + 1) * d] = (o * _silu(z_ref[bb, :, h * d:(h + 1) * d])).astype(o_ref.dtype)


def _gdn(gdn_in, gate, conv_w, alog_row, dtb_row, gn_row, nb=GDN_BATCH_PER_STEP):
    b, t, _ = gdn_in.shape
    assert b % nb == 0
    nc = t // BLK
    w_qkv = GDN_HEADS * GDN_D * 3
    w_v = GDN_HEADS * GDN_D
    fixed = lambda i, j: (0, 0)
    return pl.pallas_call(
        functools.partial(_gdn_kernel, nb=nb),
        grid=(b // nb, nc),
        in_specs=[
            pl.BlockSpec((nb, BLK, w_qkv), lambda i, j: (i, j, 0)),
            pl.BlockSpec((nb, BLK, w_v), lambda i, j: (i, j, 3)),
            pl.BlockSpec((nb, BLK, BLK), lambda i, j: (i, j, 0)),
            pl.BlockSpec(conv_w.shape, fixed),
            pl.BlockSpec(alog_row.shape, fixed),
            pl.BlockSpec(dtb_row.shape, fixed),
            pl.BlockSpec(gn_row.shape, fixed),
        ],
        out_specs=pl.BlockSpec((nb, BLK, w_v), lambda i, j: (i, j, 0)),
        out_shape=jax.ShapeDtypeStruct((b, t, w_v), BF16),
        scratch_shapes=[
            pltpu.VMEM((nb, 8, w_qkv), F32),
            pltpu.VMEM((nb, GDN_HEADS, GDN_D, GDN_D), F32),
        ],
        compiler_params=_cparams(("parallel", "arbitrary")),
        name="gdn",
    )(gdn_in, gdn_in, gate, conv_w, alog_row, dtb_row, gn_row)


def _sb_kernel(q_ref, k_ref, v_ref, o_ref, *, pad, pairs):
    c = BLK
    qi = pl.program_id(1)
    lane2 = _iota2((2 * c, c), 1)
    rel = lane2 - (_iota2((2 * c, c), 0) & (c - 1))
    upper_strict = (_iota2((c, c), 0) > _iota2((c, c), 1)).astype(BF16)
    upper_strict2 = jnp.concatenate([upper_strict, upper_strict], axis=0)
    head0 = _iota2((c, c), 1) < SB_DH
    scale = jnp.asarray(SB_DH ** -0.5, BF16)
    zero_b = jnp.zeros((c, c), BF16)

    def split_heads(x):
        return jnp.concatenate([jnp.where(head0, x, zero_b), jnp.where(head0, zero_b, x)], axis=0)

    q_cat = [split_heads(q_ref[:, p * c:(p + 1) * c] * scale) for p in range(pairs)]

    def cond(carry):
        return (carry[0] >= 0) & (carry[1] > 0)

    def body(carry):
        j0 = carry[0]
        rng = range(pairs)
        blocks = range(SB_KEY_BLOCKS)
        js = [j0 - i for i in blocks]
        starts = [pl.multiple_of(jnp.maximum(j, 0) * c, c) for j in js]
        allowed = [(rel < (qi - j) * c) & (lane2 >= pad - j * c) for j in js]
        zs = [[_dot_nt(q_cat[p], k_ref[pl.ds(starts[i], c), p * c:(p + 1) * c]) * LOG2E for p in rng] for i in blocks]
        neg_l1m = [[None] * pairs for _ in blocks]
        log_b = [[None] * pairs for _ in blocks]
        hilo = [[None] * pairs for _ in blocks]
        for i in blocks:
            for p in rng:
                t = jnp.log2(1.0 + jnp.exp2(-jnp.abs(zs[i][p])))
                nl = jnp.maximum(zs[i][p], 0.0) + t
                log_b[i][p] = zs[i][p] - nl
                nl = jnp.where(allowed[i], nl, 0.0)
                neg_l1m[i][p] = nl
                hi = nl.astype(BF16)
                hilo[i][p] = jnp.concatenate([hi, (nl - hi.astype(F32)).astype(BF16)], axis=1)
        local = [[_dot(hilo[i][p], upper_strict2) for p in rng] for i in blocks]
        out = []
        best = None
        for p in rng:
            run = carry[3 + 2 * p]
            w_cat, v_cat = [], []
            for i in blocks:
                wgt = jnp.where(allowed[i], jnp.exp2(log_b[i][p] - local[i][p] - run), 0.0).astype(BF16)
                w_cat += [wgt[:c], wgt[c:]]
                v_cat.append(split_heads(v_ref[pl.ds(starts[i], c), p * c:(p + 1) * c]))
                run = run + jnp.sum(neg_l1m[i][p], axis=-1, keepdims=True)
            acc = carry[2 + 2 * p] + _dot(jnp.concatenate(w_cat, axis=1), jnp.concatenate(v_cat, axis=0))
            out += [acc, run]
            best = run if best is None else jnp.minimum(best, run)
        live = (jnp.min(best) <= SB_UNDERFLOW_LOG2).astype(jnp.int32)
        return (j0 - SB_KEY_BLOCKS, live, *out)

    init = [qi, jnp.int32(1)]
    for p in range(pairs):
        init += [jnp.zeros((c, c), F32), jnp.zeros((2 * c, 1), F32)]
    res = lax.while_loop(cond, body, tuple(init))
    for p in range(pairs):
        o_ref[:, p * c:(p + 1) * c] = res[2 + 2 * p].astype(o_ref.dtype)


def _stick_breaking(sb_in, pad):
    b, t, _ = sb_in.shape
    nb = t // BLK
    pairs = SB_HEADS * SB_DH // BLK
    w = pairs * BLK
    return pl.pallas_call(
        functools.partial(_sb_kernel, pad=pad, pairs=pairs),
        grid=(b, nb),
        in_specs=[
            pl.BlockSpec((None, BLK, w), lambda i, j: (i, j, 0)),
            pl.BlockSpec((None, t, w), lambda i, j: (i, 0, 1)),
            pl.BlockSpec((None, t, w), lambda i, j: (i, 0, 2)),
        ],
        out_specs=pl.BlockSpec((None, BLK, w), lambda i, j: (i, j, 0)),
        out_shape=jax.ShapeDtypeStruct((b, t, w), BF16),
        compiler_params=_cparams(("parallel", "arbitrary")),
        name="stick_breaking",
    )(sb_in, sb_in, sb_in)


def _row_in_group(x, idx, group):
    c, d = x.shape
    x3 = x.reshape(c // group, group, d)
    return jnp.broadcast_to(x3[:, idx:idx + 1, :], x3.shape).reshape(c, d)


def _split_row(b, half):
    if half >= 8:
        return _row_in_group(b, half, 2 * half)
    sub = _iota2(b.shape, 0) & 7
    picks = [_row_in_group(b, s, 8) for s in range(half, 8, 2 * half)]
    out = picks[-1]
    for i in reversed(range(len(picks) - 1)):
        out = jnp.where(sub < (i + 1) * 2 * half, picks[i], out)
    return out


def _hgrn_intra(qs, ks, bs):
    c = qs[0].shape[0]
    hs = range(len(qs))
    row = _iota2((c, c), 0)
    col = _iota2((c, c), 1)
    levels = []
    half = c // 2
    while half >= 1:
        lower = (row & (2 * half - 1)) >= half
        prods = []
        for h in hs:
            e = _exp_neg(jnp.abs(bs[h] - _split_row(bs[h], half)))
            ql = jnp.where(lower, qs[h] * e, 0.0).astype(BF16)
            kl = jnp.where(lower, 0.0, ks[h] * e).astype(BF16)
            prods.append(_dot_nt(ql, kl))
        levels.append((half, prods))
        half //= 2
    diag = [jnp.sum(qs[h] * ks[h], axis=-1, keepdims=True) for h in hs]
    out = []
    for h in hs:
        a = levels[0][1][h]
        for half, prods in levels[1:]:
            shift = int(math.log2(2 * half))
            a = jnp.where((row >> shift) == (col >> shift), prods[h], a)
        out.append(jnp.where(row == col, diag[h], a))
    return out


def _hgrn_kernel(q_ref, f_ref, i_ref, z_ref, lbraw_ref, gn_ref, o_ref, s_ref, *, pad, layer, heads):
    c = BLK
    d = HG_D
    j = pl.program_id(2)
    hs = range(heads)

    @pl.when(j == 0)
    def _():
        s_ref[...] = jnp.zeros_like(s_ref)

    real = (j * c + _iota2((c, d), 0)) >= pad
    raw = lbraw_ref[...]
    ex = jnp.exp(raw - jnp.max(raw, axis=0, keepdims=True))
    sm = ex / jnp.sum(ex, axis=0, keepdims=True)
    lb_all = jnp.sum(sm[1:layer + 1, :], axis=0, keepdims=True)

    sl = [slice(h * d, (h + 1) * d) for h in hs]
    lb = [lb_all[:, sl[h]] for h in hs]
    q = [_silu(q_ref[:, sl[h]]) for h in hs]
    fgate = [lb[h] + (1.0 - lb[h]) * _sigmoid(f_ref[:, sl[h]]) for h in hs]
    k = [1.0 - fgate[h] for h in hs]
    vb = [jnp.where(real, i_ref[:, sl[h]], 0.0).astype(BF16) for h in hs]
    b = [_cumsum_rows(jnp.log(fgate[h])) for h in hs]
    b_last = [b[h][c - 1:c, :] for h in hs]
    s_t = [s_ref[h] for h in hs]
    o_state = [_dot_nt((q[h] * jnp.exp(b[h])).astype(BF16), s_t[h].astype(BF16)) for h in hs]
    a = _hgrn_intra(q, k, b)
    o_intra = [_dot(a[h].astype(BF16), vb[h]) for h in hs]
    k_dec = [(k[h] * jnp.exp(b_last[h] - b[h])).astype(BF16) for h in hs]
    s_add = [_dot_tn(vb[h], k_dec[h]) for h in hs]
    for h in hs:
        s_ref[h] = s_t[h] * jnp.exp(b_last[h]) + s_add[h]
        o = o_state[h] + o_intra[h]
        o = o * lax.rsqrt(jnp.mean(o * o, axis=-1, keepdims=True) + RMS_EPS) * gn_ref[...]
        o_ref[:, sl[h]] = (o * _silu(z_ref[:, sl[h]])).astype(o_ref.dtype)


def _hgrn2(c_in, lb_raw, gn_row, pad, layer, heads_per_step=8):
    b, t, _ = c_in.shape
    nc = t // BLK
    groups = HG_HEADS // heads_per_step
    w = heads_per_step * HG_D
    depth = lb_raw.shape[0]
    col_spec = lambda part: pl.BlockSpec((None, BLK, w), lambda i, g, j: (i, j, part * groups + g))
    return pl.pallas_call(
        functools.partial(_hgrn_kernel, pad=pad, layer=layer, heads=heads_per_step),
        grid=(b, groups, nc),
        in_specs=[
            col_spec(0), col_spec(1), col_spec(2), col_spec(3),
            pl.BlockSpec((depth, w), lambda i, g, j: (0, g)),
            pl.BlockSpec(gn_row.shape, lambda i, g, j: (0, 0)),
        ],
        out_specs=pl.BlockSpec((None, BLK, w), lambda i, g, j: (i, j, g)),
        out_shape=jax.ShapeDtypeStruct((b, t, HG_HEADS * HG_D), BF16),
        scratch_shapes=[pltpu.VMEM((heads_per_step, HG_D, HG_D), F32)],
        compiler_params=_cparams(("parallel", "parallel", "arbitrary")),
        name="hgrn2",
    )(c_in, c_in, c_in, c_in, lb_raw, gn_row)


def _lane_row(vec, offset, width=BLK):
    return jnp.zeros((1, width), F32).at[0, offset:offset + vec.shape[0]].set(vec.astype(F32))


def kernel(x, meta_tokens, ab_w_in, ab_conv_w, ab_a_log, ab_dt_bias, ab_gnorm_g, ab_w_out, c_w_in, c_lb_raw,
           c_gnorm_g, c_w_out, ln_mix_g, ln_mix_b, mlp_w1, mlp_w2, ln_ffn_g, ln_ffn_b):
    bsz, seq, d = x.shape
    n_meta = meta_tokens.shape[0]
    depth = ln_mix_g.shape[0]
    alpha = float((2 * depth) ** 0.25)
    pad = (-n_meta) % BLK
    t = pad + n_meta + seq
    assert seq % BLK == 0 and depth == 2
    m = bsz * t
    tm = 512
    assert m % tm == 0

    meta = jnp.broadcast_to(meta_tokens[None].astype(x.dtype), (bsz, n_meta, d))
    h = jnp.concatenate([jnp.zeros((bsz, pad, d), x.dtype), meta, x], axis=1).reshape(m, d)

    w_gdn = GDN_HEADS * GDN_D * 4
    w_sb = SB_HEADS * SB_DH * 3
    row2 = lambda a: a.astype(F32).reshape(1, -1)

    w_in = ab_w_in[0]
    wa = w_in[:, :w_gdn].astype(BF16)
    wg = jnp.pad(w_in[:, w_gdn:w_gdn + 2 * GDN_HEADS], ((0, 0), (0, BLK - 2 * GDN_HEADS))).astype(BF16)
    wb = w_in[:, w_gdn + 2 * GDN_HEADS:].astype(BF16)
    gdn_in, gate, sb_in = _project(h, [wa, wg, wb], [F32, F32, BF16], tm)
    oa = _gdn(gdn_in.reshape(bsz, t, w_gdn), gate.reshape(bsz, t, BLK), ab_conv_w[0].astype(F32),
              _lane_row(ab_a_log[0], GDN_HEADS), _lane_row(ab_dt_bias[0], GDN_HEADS), row2(ab_gnorm_g[0]))
    ob = _stick_breaking(sb_in.reshape(bsz, t, w_sb), pad)
    w_out = ab_w_out[0].astype(BF16)
    n_a = GDN_HEADS * GDN_D
    h = _post_mixer([oa.reshape(m, n_a), ob.reshape(m, -1)], [w_out[:n_a], w_out[n_a:]], h,
                    row2(ln_mix_g[0]), row2(ln_mix_b[0]), mlp_w1[0].astype(BF16), mlp_w2[0].astype(BF16),
                    row2(ln_ffn_g[0]), row2(ln_ffn_b[0]), alpha, tm)

    (c_in,) = _project(h, [c_w_in[0].astype(BF16)], [F32], tm)
    oc = _hgrn2(c_in.reshape(bsz, t, -1), c_lb_raw.astype(F32), row2(c_gnorm_g[0]), pad, 1)
    out = _post_mixer([oc.reshape(m, -1)], [c_w_out[0].astype(BF16)], h,
                      row2(ln_mix_g[1]), row2(ln_mix_b[1]), mlp_w1[1].astype(BF16), mlp_w2[1].astype(BF16),
                      row2(ln_ffn_g[1]), row2(ln_ffn_b[1]), alpha, tm, keep=(bsz, t, pad + n_meta, seq))
    return out.reshape(bsz, seq, d)
```

```python
import functools
import math

import jax
import jax.numpy as jnp
from jax import lax
from jax.experimental import pallas as pl
from jax.experimental.pallas import tpu as pltpu

F32 = jnp.float32
BF16 = jnp.bfloat16

BLK = 128
LN_EPS = 1e-5
RMS_EPS = 1e-6
L2_EPS = 1e-6
GDN_HEADS = 4
GDN_D = 128
GDN_CONV = 4
GDN_BATCH_PER_STEP = 4
SB_HEADS = 8
SB_DH = 64
HG_HEADS = 8
HG_D = 128
SB_UNDERFLOW_LOG2 = 151.0
LOG2E = 1.4426950408889634
SB_KEY_BLOCKS = 3
VMEM_LIMIT = 56 * 1024 * 1024


def _cparams(sem):
    return pltpu.CompilerParams(dimension_semantics=sem, vmem_limit_bytes=VMEM_LIMIT)


def _dot(a, b):
    return jnp.dot(a, b, preferred_element_type=F32)


def _dot_nt(a, b):
    return lax.dot_general(a, b, (((1,), (1,)), ((), ())), preferred_element_type=F32)


def _dot_tn(a, b):
    return lax.dot_general(a, b, (((0,), (0,)), ((), ())), preferred_element_type=F32)


def _exp_neg(x):
    return jnp.exp2(x * (-LOG2E))


def _sigmoid(x):
    return 1.0 / (1.0 + _exp_neg(x))


def _silu(x):
    return x * _sigmoid(x)


def _softplus(x):
    return jnp.maximum(x, 0.0) + jnp.log(1.0 + jnp.exp(-jnp.abs(x)))


def _iota2(shape, dim):
    return lax.broadcasted_iota(jnp.int32, shape, dim)


def _layer_norm(x, g, b):
    mu = jnp.mean(x, axis=-1, keepdims=True)
    xc = x - mu
    var = jnp.mean(xc * xc, axis=-1, keepdims=True)
    return xc * lax.rsqrt(var + LN_EPS) * g + b


def _proj_kernel(x_ref, *refs, n_out):
    w_refs, o_refs = refs[:n_out], refs[n_out:]
    xb = x_ref[...].astype(BF16)
    for w_ref, o_ref in zip(w_refs, o_refs):
        o_ref[...] = _dot(xb, w_ref[...]).astype(o_ref.dtype)


def _project(x2d, weights, out_dtypes, tm):
    m, d = x2d.shape
    n_out = len(weights)
    in_specs = [pl.BlockSpec((tm, d), lambda i: (i, 0))]
    for w in weights:
        in_specs.append(pl.BlockSpec(w.shape, lambda i: (0, 0), pipeline_mode=pl.Buffered(1)))
    out_specs = [pl.BlockSpec((tm, w.shape[1]), lambda i: (i, 0)) for w in weights]
    out_shape = [jax.ShapeDtypeStruct((m, w.shape[1]), dt) for w, dt in zip(weights, out_dtypes)]
    return pl.pallas_call(
        functools.partial(_proj_kernel, n_out=n_out),
        grid=(m // tm,),
        in_specs=in_specs,
        out_specs=out_specs,
        out_shape=out_shape,
        compiler_params=_cparams(("parallel",)),
        name="proj",
    )(x2d, *weights)


def _post_kernel(*refs, n_mix, alpha, ff_chunks):
    y_refs = refs[:n_mix]
    wo_refs = refs[n_mix:2 * n_mix]
    h_ref, g1_ref, b1_ref, w1_ref, w2_ref, g2_ref, b2_ref, o_ref = refs[2 * n_mix:]
    mix = _dot(y_refs[0][...], wo_refs[0][...])
    for y_ref, wo_ref in zip(y_refs[1:], wo_refs[1:]):
        mix = mix + _dot(y_ref[...], wo_ref[...])
    h1 = _layer_norm(alpha * h_ref[...] + mix, g1_ref[...], b1_ref[...])
    h1b = h1.astype(BF16)
    d_ff = w1_ref.shape[1]
    fc = d_ff // ff_chunks
    acc = None
    for j in range(ff_chunks):
        hid = _dot(h1b, w1_ref[:, j * fc:(j + 1) * fc])
        hid = jnp.square(jnp.maximum(hid, 0.0)).astype(BF16)
        part = _dot(hid, w2_ref[j * fc:(j + 1) * fc, :])
        acc = part if acc is None else acc + part
    o_ref[...] = _layer_norm(alpha * h1 + acc, g2_ref[...], b2_ref[...])


def _post_mixer(ys, wos, h2d, g1, b1, w1, w2, g2, b2, alpha, tm, keep=None):
    m, d = h2d.shape
    n_mix = len(ys)
    const = lambda a: pl.BlockSpec(a.shape, lambda *_: (0, 0), pipeline_mode=pl.Buffered(1))
    if keep is None:
        grid = (m // tm,)
        row_in = lambda w: pl.BlockSpec((tm, w), lambda i: (i, 0))
        out_spec = pl.BlockSpec((tm, d), lambda i: (i, 0))
        m_out = m
    else:
        bsz, t, first, kept = keep
        assert kept % tm == 0 and first % 8 == 0 and t % 8 == 0
        grid = (bsz, kept // tm)
        row_in = lambda w: pl.BlockSpec((pl.Element(tm), pl.Element(w)), lambda b, i: (pl.multiple_of(b * t + first + i * tm, 8), 0))
        out_spec = pl.BlockSpec((tm, d), lambda b, i: (b * (kept // tm) + i, 0))
        m_out = bsz * kept
    in_specs = [row_in(y.shape[1]) for y in ys]
    in_specs += [const(w) for w in wos]
    in_specs += [row_in(d), const(g1), const(b1), const(w1), const(w2), const(g2), const(b2)]
    return pl.pallas_call(
        functools.partial(_post_kernel, n_mix=n_mix, alpha=alpha, ff_chunks=4),
        grid=grid,
        in_specs=in_specs,
        out_specs=out_spec,
        out_shape=jax.ShapeDtypeStruct((m_out, d), F32),
        compiler_params=_cparams(("parallel",) * len(grid)),
        name="post_mixer",
    )(*ys, *wos, h2d, g1, b1, w1, w2, g2, b2)


def _split3(x):
    hi = x.astype(BF16)
    r1 = x - hi.astype(F32)
    mid = r1.astype(BF16)
    lo = (r1 - mid.astype(F32)).astype(BF16)
    return hi, mid, lo


def _cumsum_rows(x):
    c = x.shape[0]
    tril = (_iota2((c, c), 0) >= _iota2((c, c), 1)).astype(BF16)
    return _dot(jnp.concatenate([tril, tril, tril], axis=1), jnp.concatenate(_split3(x), axis=0))


def _unit_lower_inverses(ms):
    c = ms[0].shape[0]
    assert c == 128
    eye = (_iota2((c, c), 0) == _iota2((c, c), 1)).astype(F32)
    bf = lambda xs: [x.astype(BF16) for x in xs]
    sq = lambda xs: [_dot(x, x) for x in xs]
    mul = lambda xs, ys: [_dot(x, y) for x, y in zip(xs, ys)]
    x1 = [-m for m in ms]
    x1b = bf(x1)
    x2 = sq(x1b)
    x2b = bf(x2)
    x4 = sq(x2b)
    x12 = mul(x1b, x2b)
    x4b = bf(x4)
    x8 = sq(x4b)
    g01 = [eye + a + b + ab for a, b, ab in zip(x1, x2, x12)]
    x8b = bf(x8)
    x16 = sq(x8b)
    x48 = mul(x4b, x8b)
    x16b = bf(x16)
    g23 = [eye + a + b + ab for a, b, ab in zip(x4, x8, x48)]
    x32 = sq(x16b)
    g0123 = mul(bf(g01), bf(g23))
    x32b = bf(x32)
    x64 = sq(x32b)
    x1632 = mul(x16b, x32b)
    g45 = [eye + a + b + ab for a, b, ab in zip(x16, x32, x1632)]
    g456 = [g + gx for g, gx in zip(g45, mul(bf(g45), bf(x64)))]
    return mul(bf(g0123), bf(g456))


def _gdn_kernel(qkv_ref, z_ref, gate_ref, convw_ref, alog_ref, dtb_ref, gn_ref, o_ref, xc_ref, s_ref, *, nb):
    c = BLK
    nh = GDN_HEADS
    d = GDN_D
    w_qkv = nh * d * 3
    chains = [(bb, h) for bb in range(nb) for h in range(nh)]
    ns = range(len(chains))

    @pl.when(pl.program_id(1) == 0)
    def _():
        xc_ref[...] = jnp.zeros_like(xc_ref)
        s_ref[...] = jnp.zeros_like(s_ref)

    row = _iota2((c, c), 0)
    col = _iota2((c, c), 1)
    causal = row >= col
    strict = row > col
    qkv, beta_all, gc_all, gc_all_t = [], [], [], []
    for bb in range(nb):
        x = qkv_ref[bb]
        tail = xc_ref[bb]
        sub8 = _iota2((8, w_qkv), 0)
        conv = x * convw_ref[GDN_CONV - 1:GDN_CONV, :]
        for k in range(1, GDN_CONV):
            rolled = pltpu.roll(x, k, 0)
            head = jnp.where(sub8 < k, pltpu.roll(tail, k, 0), rolled[0:8])
            shifted = jnp.concatenate([head, rolled[8:]], axis=0)
            conv = conv + shifted * convw_ref[GDN_CONV - 1 - k:GDN_CONV - k, :]
        xc_ref[bb] = x[c - 8:c]
        qkv.append(_silu(conv))
        gate = gate_ref[bb]
        beta_all.append(_sigmoid(gate))
        g_all = -jnp.exp(alog_ref[...]) * _softplus(gate + dtb_ref[...])
        gc_all.append(_cumsum_rows(g_all))
        gc_all_t.append(gc_all[bb].T)

    def l2n(x):
        return x * lax.rsqrt(jnp.sum(x * x, axis=-1, keepdims=True) + L2_EPS)

    q = [l2n(qkv[bb][:, h * d:(h + 1) * d]) * (d ** -0.5) for bb, h in chains]
    k = [l2n(qkv[bb][:, (nh + h) * d:(nh + h + 1) * d]) for bb, h in chains]
    v = [qkv[bb][:, (2 * nh + h) * d:(2 * nh + h + 1) * d] for bb, h in chains]
    beta = [beta_all[bb][:, h:h + 1] for bb, h in chains]
    gc = [gc_all[bb][:, nh + h:nh + h + 1] for bb, h in chains]
    gc_last = [gc_all[bb][c - 1:c, nh + h:nh + h + 1] for bb, h in chains]
    gc_row = [gc_all_t[bb][nh + h:nh + h + 1, :] for bb, h in chains]
    decay = [jnp.where(causal, jnp.exp(jnp.minimum(gc[n] - gc_row[n], 0.0)), 0.0) for n in ns]
    kb = [k[n] * beta[n] for n in ns]
    kbf = [k[n].astype(BF16) for n in ns]
    kk = [_dot_nt(kb[n].astype(BF16), kbf[n]) for n in ns]
    qk = [_dot_nt(q[n].astype(BF16), kbf[n]) for n in ns]
    t_inv = _unit_lower_inverses([jnp.where(strict, kk[n] * decay[n], 0.0) for n in ns])
    t_invb = [t.astype(BF16) for t in t_inv]
    egc = [jnp.exp(gc[n]) for n in ns]
    u = [_dot(t_invb[n], (v[n] * beta[n]).astype(BF16)) for n in ns]
    w = [_dot(t_invb[n], (kb[n] * egc[n]).astype(BF16)).astype(BF16) for n in ns]
    a_intra = [(qk[n] * decay[n]).astype(BF16) for n in ns]
    q_dec = [(q[n] * egc[n]).astype(BF16) for n in ns]
    k_dec = [(k[n] * jnp.exp(gc_last[n] - gc[n])).astype(BF16) for n in ns]
    s = [s_ref[bb, h] for bb, h in chains]
    sb = [s[n].astype(BF16) for n in ns]
    v_new = [(u[n] - _dot(w[n], sb[n])).astype(BF16) for n in ns]
    o_state = [_dot(q_dec[n], sb[n]) for n in ns]
    o_intra = [_dot(a_intra[n], v_new[n]) for n in ns]
    s_add = [_dot_tn(k_dec[n], v_new[n]) for n in ns]
    for n, (bb, h) in enumerate(chains):
        s_ref[bb, h] = s[n] * jnp.exp(gc_last[n]) + s_add[n]
        o = o_state[n] + o_intra[n]
        o = o * lax.rsqrt(jnp.mean(o * o, axis=-1, keepdims=True) + RMS_EPS) * gn_ref[...]
        o_ref[bb, :, h * d:(h + 1) * d] = (o * _silu(z_ref[bb, :, h * d:(h + 1) * d])).astype(o_ref.dtype)


def _gdn(gdn_in, gate, conv_w, alog_row, dtb_row, gn_row, nb=GDN_BATCH_PER_STEP):
    b, t, _ = gdn_in.shape
    assert b % nb == 0
    nc = t // BLK
    w_qkv = GDN_HEADS * GDN_D * 3
    w_v = GDN_HEADS * GDN_D
    fixed = lambda i, j: (0, 0)
    return pl.pallas_call(
        functools.partial(_gdn_kernel, nb=nb),
        grid=(b // nb, nc),
        in_specs=[
            pl.BlockSpec((nb, BLK, w_qkv), lambda i, j: (i, j, 0)),
            pl.BlockSpec((nb, BLK, w_v), lambda i, j: (i, j, 3)),
            pl.BlockSpec((nb, BLK, BLK), lambda i, j: (i, j, 0)),
            pl.BlockSpec(conv_w.shape, fixed),
            pl.BlockSpec(alog_row.shape, fixed),
            pl.BlockSpec(dtb_row.shape, fixed),
            pl.BlockSpec(gn_row.shape, fixed),
        ],
        out_specs=pl.BlockSpec((nb, BLK, w_v), lambda i, j: (i, j, 0)),
        out_shape=jax.ShapeDtypeStruct((b, t, w_v), BF16),
        scratch_shapes=[
            pltpu.VMEM((nb, 8, w_qkv), F32),
            pltpu.VMEM((nb, GDN_HEADS, GDN_D, GDN_D), F32),
        ],
        compiler_params=_cparams(("parallel", "arbitrary")),
        name="gdn",
    )(gdn_in, gdn_in, gate, conv_w, alog_row, dtb_row, gn_row)


def _sb_kernel(q_ref, k_ref, v_ref, o_ref, *, pairs):
    c = BLK
    qi = pl.program_id(1)
    rel = _iota2((2 * c, c), 1) - (_iota2((2 * c, c), 0) & (c - 1))
    upper_strict = (_iota2((c, c), 0) > _iota2((c, c), 1)).astype(BF16)
    upper_strict2 = jnp.concatenate([upper_strict, upper_strict], axis=0)
    head0 = _iota2((c, c), 1) < SB_DH
    scale = jnp.asarray(SB_DH ** -0.5, BF16)
    zero_b = jnp.zeros((c, c), BF16)

    def split_heads(x):
        return jnp.concatenate([jnp.where(head0, x, zero_b), jnp.where(head0, zero_b, x)], axis=0)

    q_cat = [split_heads(q_ref[:, p * c:(p + 1) * c] * scale) for p in range(pairs)]

    def cond(carry):
        return (carry[0] >= 0) & (carry[1] > 0)

    def body(carry):
        j0 = carry[0]
        rng = range(pairs)
        blocks = range(SB_KEY_BLOCKS)
        js = [j0 - i for i in blocks]
        starts = [pl.multiple_of(jnp.maximum(j, 0) * c, c) for j in js]
        diag_ok = rel < (qi - j0) * c
        zs = [[_dot_nt(q_cat[p], k_ref[pl.ds(starts[i], c), p * c:(p + 1) * c]) * LOG2E for p in rng] for i in blocks]
        neg_l1m = [[None] * pairs for _ in blocks]
        log_b = [[None] * pairs for _ in blocks]
        hilo = [[None] * pairs for _ in blocks]
        for i in blocks:
            for p in rng:
                t = jnp.log2(1.0 + jnp.exp2(-jnp.abs(zs[i][p])))
                nl = jnp.maximum(zs[i][p], 0.0) + t
                log_b[i][p] = zs[i][p] - nl
                if i == 0:
                    nl = jnp.where(diag_ok, nl, 0.0)
                neg_l1m[i][p] = nl
                hi = nl.astype(BF16)
                hilo[i][p] = jnp.concatenate([hi, (nl - hi.astype(F32)).astype(BF16)], axis=1)
        local = [[_dot(hilo[i][p], upper_strict2) for p in rng] for i in blocks]
        out = []
        best = None
        for p in rng:
            run = carry[3 + 2 * p]
            w_cat, v_cat = [], []
            for i in blocks:
                wgt = jnp.exp2(log_b[i][p] - local[i][p] - run)
                if i == 0:
                    wgt = jnp.where(diag_ok, wgt, 0.0)
                wgt = wgt.astype(BF16)
                w_cat += [wgt[:c], wgt[c:]]
                vj = v_ref[pl.ds(starts[i], c), p * c:(p + 1) * c]
                if i > 0:
                    vj = jnp.where(js[i] >= 0, vj, zero_b)
                v_cat.append(split_heads(vj))
                run = run + jnp.sum(neg_l1m[i][p], axis=-1, keepdims=True)
            acc = carry[2 + 2 * p] + _dot(jnp.concatenate(w_cat, axis=1), jnp.concatenate(v_cat, axis=0))
            out += [acc, run]
            best = run if best is None else jnp.minimum(best, run)
        live = (jnp.min(best) <= SB_UNDERFLOW_LOG2).astype(jnp.int32)
        return (j0 - SB_KEY_BLOCKS, live, *out)

    init = [qi, jnp.int32(1)]
    for p in range(pairs):
        init += [jnp.zeros((c, c), F32), jnp.zeros((2 * c, 1), F32)]
    res = lax.while_loop(cond, body, tuple(init))
    for p in range(pairs):
        o_ref[:, p * c:(p + 1) * c] = res[2 + 2 * p].astype(o_ref.dtype)


def _stick_breaking(sb_in):
    b, t, _ = sb_in.shape
    nb = t // BLK
    pairs = SB_HEADS * SB_DH // BLK
    w = pairs * BLK
    return pl.pallas_call(
        functools.partial(_sb_kernel, pairs=pairs),
        grid=(b, nb),
        in_specs=[
            pl.BlockSpec((None, BLK, w), lambda i, j: (i, j, 0)),
            pl.BlockSpec((None, t, w), lambda i, j: (i, 0, 1)),
            pl.BlockSpec((None, t, w), lambda i, j: (i, 0, 2)),
        ],
        out_specs=pl.BlockSpec((None, BLK, w), lambda i, j: (i, j, 0)),
        out_shape=jax.ShapeDtypeStruct((b, t, w), BF16),
        compiler_params=_cparams(("parallel", "arbitrary")),
        name="stick_breaking",
    )(sb_in, sb_in, sb_in)


def _row_in_group(x, idx, group):
    c, d = x.shape
    x3 = x.reshape(c // group, group, d)
    return jnp.broadcast_to(x3[:, idx:idx + 1, :], x3.shape).reshape(c, d)


def _split_row(b, half):
    if half >= 8:
        return _row_in_group(b, half, 2 * half)
    sub = _iota2(b.shape, 0) & 7
    picks = [_row_in_group(b, s, 8) for s in range(half, 8, 2 * half)]
    out = picks[-1]
    for i in reversed(range(len(picks) - 1)):
        out = jnp.where(sub < (i + 1) * 2 * half, picks[i], out)
    return out


def _hgrn_intra(qs, ks, bs):
    c = qs[0].shape[0]
    hs = range(len(qs))
    row = _iota2((c, c), 0)
    col = _iota2((c, c), 1)
    levels = []
    half = c // 2
    while half >= 1:
        lower = (row & (2 * half - 1)) >= half
        scale = jnp.where(lower, LOG2E, -LOG2E)
        prods = []
        for h in hs:
            e = jnp.exp2((bs[h] - _split_row(bs[h], half)) * scale)
            ql = jnp.where(lower, qs[h] * e, 0.0).astype(BF16)
            kl = jnp.where(lower, 0.0, ks[h] * e).astype(BF16)
            prods.append(_dot_nt(ql, kl))
        levels.append((half, prods))
        half //= 2
    diag = [jnp.sum(qs[h] * ks[h], axis=-1, keepdims=True) for h in hs]
    out = []
    for h in hs:
        a = levels[0][1][h]
        for half, prods in levels[1:]:
            shift = int(math.log2(2 * half))
            a = jnp.where((row >> shift) == (col >> shift), prods[h], a)
        out.append(jnp.where(row == col, diag[h], a))
    return out


def _hgrn_kernel(q_ref, f_ref, i_ref, z_ref, lbraw_ref, gn_ref, o_ref, s_ref, *, pad, layer, heads):
    c = BLK
    d = HG_D
    j = pl.program_id(2)
    hs = range(heads)

    @pl.when(j == 0)
    def _():
        s_ref[...] = jnp.zeros_like(s_ref)

    real = (j * c + _iota2((c, d), 0)) >= pad
    raw = lbraw_ref[...]
    ex = jnp.exp(raw - jnp.max(raw, axis=0, keepdims=True))
    sm = ex / jnp.sum(ex, axis=0, keepdims=True)
    lb_all = jnp.sum(sm[1:layer + 1, :], axis=0, keepdims=True)

    sl = [slice(h * d, (h + 1) * d) for h in hs]
    lb = [lb_all[:, sl[h]] for h in hs]
    q = [_silu(q_ref[:, sl[h]]) for h in hs]
    fgate = [lb[h] + (1.0 - lb[h]) * _sigmoid(f_ref[:, sl[h]]) for h in hs]
    k = [1.0 - fgate[h] for h in hs]
    vb = [jnp.where(real, i_ref[:, sl[h]], 0.0).astype(BF16) for h in hs]
    b = [_cumsum_rows(jnp.log(fgate[h])) for h in hs]
    b_last = [b[h][c - 1:c, :] for h in hs]
    s_t = [s_ref[h] for h in hs]
    o_state = [_dot_nt((q[h] * jnp.exp(b[h])).astype(BF16), s_t[h].astype(BF16)) for h in hs]
    a = _hgrn_intra(q, k, b)
    o_intra = [_dot(a[h].astype(BF16), vb[h]) for h in hs]
    k_dec = [(k[h] * jnp.exp(b_last[h] - b[h])).astype(BF16) for h in hs]
    s_add = [_dot_tn(vb[h], k_dec[h]) for h in hs]
    for h in hs:
        s_ref[h] = s_t[h] * jnp.exp(b_last[h]) + s_add[h]
        o = o_state[h] + o_intra[h]
        o = o * lax.rsqrt(jnp.mean(o * o, axis=-1, keepdims=True) + RMS_EPS) * gn_ref[...]
        o_ref[:, sl[h]] = (o * _silu(z_ref[:, sl[h]])).astype(o_ref.dtype)


def _hgrn2(c_in, lb_raw, gn_row, pad, layer, heads_per_step=8):
    b, t, _ = c_in.shape
    nc = t // BLK
    groups = HG_HEADS // heads_per_step
    w = heads_per_step * HG_D
    depth = lb_raw.shape[0]
    col_spec = lambda part: pl.BlockSpec((None, BLK, w), lambda i, g, j: (i, j, part * groups + g))
    return pl.pallas_call(
        functools.partial(_hgrn_kernel, pad=pad, layer=layer, heads=heads_per_step),
        grid=(b, groups, nc),
        in_specs=[
            col_spec(0), col_spec(1), col_spec(2), col_spec(3),
            pl.BlockSpec((depth, w), lambda i, g, j: (0, g)),
            pl.BlockSpec(gn_row.shape, lambda i, g, j: (0, 0)),
        ],
        out_specs=pl.BlockSpec((None, BLK, w), lambda i, g, j: (i, j, g)),
        out_shape=jax.ShapeDtypeStruct((b, t, HG_HEADS * HG_D), BF16),
        scratch_shapes=[pltpu.VMEM((heads_per_step, HG_D, HG_D), F32)],
        compiler_params=_cparams(("parallel", "parallel", "arbitrary")),
        name="hgrn2",
    )(c_in, c_in, c_in, c_in, lb_raw, gn_row)


def _lane_row(vec, offset, width=BLK):
    return jnp.zeros((1, width), F32).at[0, offset:offset + vec.shape[0]].set(vec.astype(F32))


def kernel(x, meta_tokens, ab_w_in, ab_conv_w, ab_a_log, ab_dt_bias, ab_gnorm_g, ab_w_out, c_w_in, c_lb_raw,
           c_gnorm_g, c_w_out, ln_mix_g, ln_mix_b, mlp_w1, mlp_w2, ln_ffn_g, ln_ffn_b):
    bsz, seq, d = x.shape
    n_meta = meta_tokens.shape[0]
    depth = ln_mix_g.shape[0]
    alpha = float((2 * depth) ** 0.25)
    pad = (-n_meta) % BLK
    t = pad + n_meta + seq
    assert seq % BLK == 0 and depth == 2
    m = bsz * t
    tm = 512
    assert m % tm == 0

    meta = jnp.broadcast_to(meta_tokens[None].astype(x.dtype), (bsz, n_meta, d))
    h = jnp.concatenate([jnp.zeros((bsz, pad, d), x.dtype), meta, x], axis=1).reshape(m, d)

    w_gdn = GDN_HEADS * GDN_D * 4
    w_sb = SB_HEADS * SB_DH * 3
    row2 = lambda a: a.astype(F32).reshape(1, -1)

    w_in = ab_w_in[0]
    wa = w_in[:, :w_gdn].astype(BF16)
    wg = jnp.pad(w_in[:, w_gdn:w_gdn + 2 * GDN_HEADS], ((0, 0), (0, BLK - 2 * GDN_HEADS))).astype(BF16)
    wb = w_in[:, w_gdn + 2 * GDN_HEADS:].astype(BF16)
    gdn_in, gate, sb_in = _project(h, [wa, wg, wb], [F32, F32, BF16], tm)
    oa = _gdn(gdn_in.reshape(bsz, t, w_gdn), gate.reshape(bsz, t, BLK), ab_conv_w[0].astype(F32),
              _lane_row(ab_a_log[0], GDN_HEADS), _lane_row(ab_dt_bias[0], GDN_HEADS), row2(ab_gnorm_g[0]))
    ob = _stick_breaking(sb_in.reshape(bsz, t, w_sb))
    w_out = ab_w_out[0].astype(BF16)
    n_a = GDN_HEADS * GDN_D
    h = _post_mixer([oa.reshape(m, n_a), ob.reshape(m, -1)], [w_out[:n_a], w_out[n_a:]], h,
                    row2(ln_mix_g[0]), row2(ln_mix_b[0]), mlp_w1[0].astype(BF16), mlp_w2[0].astype(BF16),
                    row2(ln_ffn_g[0]), row2(ln_ffn_b[0]), alpha, tm)

    (c_in,) = _project(h, [c_w_in[0].astype(BF16)], [F32], tm)
    oc = _hgrn2(c_in.reshape(bsz, t, -1), c_lb_raw.astype(F32), row2(c_gnorm_g[0]), pad, 1)
    out = _post_mixer([oc.reshape(m, -1)], [c_w_out[0].astype(BF16)], h,
                      row2(ln_mix_g[1]), row2(ln_mix_b[1]), mlp_w1[1].astype(BF16), mlp_w2[1].astype(BF16),
                      row2(ln_ffn_g[1]), row2(ln_ffn_b[1]), alpha, tm, keep=(bsz, t, pad + n_meta, seq))
    return out.reshape(bsz, seq, d)
```

```python
import functools
import math

import jax
import jax.numpy as jnp
from jax import lax
from jax.experimental import pallas as pl
from jax.experimental.pallas import tpu as pltpu

F32 = jnp.float32
BF16 = jnp.bfloat16

BLK = 128
LN_EPS = 1e-5
RMS_EPS = 1e-6
L2_EPS = 1e-6
GDN_HEADS = 4
GDN_D = 128
GDN_CONV = 4
GDN_BATCH_PER_STEP = 4
SB_HEADS = 8
SB_DH = 64
HG_HEADS = 8
HG_D = 128
SB_UNDERFLOW_LOG2 = 151.0
LOG2E = 1.4426950408889634
SB_KEY_BLOCKS = 3
SB_QUERY_BLOCKS = 3
VMEM_LIMIT = 56 * 1024 * 1024


def _cparams(sem):
    return pltpu.CompilerParams(dimension_semantics=sem, vmem_limit_bytes=VMEM_LIMIT)


def _dot(a, b):
    return jnp.dot(a, b, preferred_element_type=F32)


def _dot_nt(a, b):
    return lax.dot_general(a, b, (((1,), (1,)), ((), ())), preferred_element_type=F32)


def _dot_tn(a, b):
    return lax.dot_general(a, b, (((0,), (0,)), ((), ())), preferred_element_type=F32)


def _exp_neg(x):
    return jnp.exp2(x * (-LOG2E))


def _sigmoid(x):
    return 1.0 / (1.0 + _exp_neg(x))


def _silu(x):
    return x * _sigmoid(x)


def _softplus(x):
    return jnp.maximum(x, 0.0) + jnp.log(1.0 + jnp.exp(-jnp.abs(x)))


def _iota2(shape, dim):
    return lax.broadcasted_iota(jnp.int32, shape, dim)


def _layer_norm(x, g, b):
    mu = jnp.mean(x, axis=-1, keepdims=True)
    xc = x - mu
    var = jnp.mean(xc * xc, axis=-1, keepdims=True)
    return xc * lax.rsqrt(var + LN_EPS) * g + b


def _proj_kernel(x_ref, *refs, n_out):
    w_refs, o_refs = refs[:n_out], refs[n_out:]
    xb = x_ref[...].astype(BF16)
    for w_ref, o_ref in zip(w_refs, o_refs):
        o_ref[...] = _dot(xb, w_ref[...]).astype(o_ref.dtype)


def _project(x2d, weights, out_dtypes, tm):
    m, d = x2d.shape
    n_out = len(weights)
    in_specs = [pl.BlockSpec((tm, d), lambda i: (i, 0))]
    for w in weights:
        in_specs.append(pl.BlockSpec(w.shape, lambda i: (0, 0), pipeline_mode=pl.Buffered(1)))
    out_specs = [pl.BlockSpec((tm, w.shape[1]), lambda i: (i, 0)) for w in weights]
    out_shape = [jax.ShapeDtypeStruct((m, w.shape[1]), dt) for w, dt in zip(weights, out_dtypes)]
    return pl.pallas_call(
        functools.partial(_proj_kernel, n_out=n_out),
        grid=(m // tm,),
        in_specs=in_specs,
        out_specs=out_specs,
        out_shape=out_shape,
        compiler_params=_cparams(("parallel",)),
        name="proj",
    )(x2d, *weights)


def _post_kernel(*refs, n_mix, alpha, ff_chunks):
    y_refs = refs[:n_mix]
    wo_refs = refs[n_mix:2 * n_mix]
    h_ref, g1_ref, b1_ref, w1_ref, w2_ref, g2_ref, b2_ref, o_ref = refs[2 * n_mix:]
    mix = _dot(y_refs[0][...], wo_refs[0][...])
    for y_ref, wo_ref in zip(y_refs[1:], wo_refs[1:]):
        mix = mix + _dot(y_ref[...], wo_ref[...])
    h1 = _layer_norm(alpha * h_ref[...] + mix, g1_ref[...], b1_ref[...])
    h1b = h1.astype(BF16)
    d_ff = w1_ref.shape[1]
    fc = d_ff // ff_chunks
    acc = None
    for j in range(ff_chunks):
        hid = _dot(h1b, w1_ref[:, j * fc:(j + 1) * fc])
        hid = jnp.square(jnp.maximum(hid, 0.0)).astype(BF16)
        part = _dot(hid, w2_ref[j * fc:(j + 1) * fc, :])
        acc = part if acc is None else acc + part
    o_ref[...] = _layer_norm(alpha * h1 + acc, g2_ref[...], b2_ref[...])


def _post_mixer(ys, wos, h2d, g1, b1, w1, w2, g2, b2, alpha, tm, keep=None):
    m, d = h2d.shape
    n_mix = len(ys)
    const = lambda a: pl.BlockSpec(a.shape, lambda *_: (0, 0), pipeline_mode=pl.Buffered(1))
    if keep is None:
        grid = (m // tm,)
        row_in = lambda w: pl.BlockSpec((tm, w), lambda i: (i, 0))
        out_spec = pl.BlockSpec((tm, d), lambda i: (i, 0))
        m_out = m
    else:
        bsz, t, first, kept = keep
        assert kept % tm == 0 and first % 8 == 0 and t % 8 == 0
        grid = (bsz, kept // tm)
        row_in = lambda w: pl.BlockSpec((pl.Element(tm), pl.Element(w)), lambda b, i: (pl.multiple_of(b * t + first + i * tm, 8), 0))
        out_spec = pl.BlockSpec((tm, d), lambda b, i: (b * (kept // tm) + i, 0))
        m_out = bsz * kept
    in_specs = [row_in(y.shape[1]) for y in ys]
    in_specs += [const(w) for w in wos]
    in_specs += [row_in(d), const(g1), const(b1), const(w1), const(w2), const(g2), const(b2)]
    return pl.pallas_call(
        functools.partial(_post_kernel, n_mix=n_mix, alpha=alpha, ff_chunks=4),
        grid=grid,
        in_specs=in_specs,
        out_specs=out_spec,
        out_shape=jax.ShapeDtypeStruct((m_out, d), F32),
        compiler_params=_cparams(("parallel",) * len(grid)),
        name="post_mixer",
    )(*ys, *wos, h2d, g1, b1, w1, w2, g2, b2)


def _split3(x):
    hi = x.astype(BF16)
    r1 = x - hi.astype(F32)
    mid = r1.astype(BF16)
    lo = (r1 - mid.astype(F32)).astype(BF16)
    return hi, mid, lo


def _cumsum_rows(x):
    c = x.shape[0]
    tril = (_iota2((c, c), 0) >= _iota2((c, c), 1)).astype(BF16)
    return _dot(jnp.concatenate([tril, tril, tril], axis=1), jnp.concatenate(_split3(x), axis=0))


def _unit_lower_inverses(ms):
    c = ms[0].shape[0]
    assert c == 128
    eye = (_iota2((c, c), 0) == _iota2((c, c), 1)).astype(F32)
    bf = lambda xs: [x.astype(BF16) for x in xs]
    sq = lambda xs: [_dot(x, x) for x in xs]
    mul = lambda xs, ys: [_dot(x, y) for x, y in zip(xs, ys)]
    x1 = [-m for m in ms]
    x1b = bf(x1)
    x2 = sq(x1b)
    x2b = bf(x2)
    x4 = sq(x2b)
    x12 = mul(x1b, x2b)
    x4b = bf(x4)
    x8 = sq(x4b)
    g01 = [eye + a + b + ab for a, b, ab in zip(x1, x2, x12)]
    x8b = bf(x8)
    x16 = sq(x8b)
    x48 = mul(x4b, x8b)
    x16b = bf(x16)
    g23 = [eye + a + b + ab for a, b, ab in zip(x4, x8, x48)]
    x32 = sq(x16b)
    g0123 = mul(bf(g01), bf(g23))
    x32b = bf(x32)
    x64 = sq(x32b)
    x1632 = mul(x16b, x32b)
    g45 = [eye + a + b + ab for a, b, ab in zip(x16, x32, x1632)]
    g456 = [g + gx for g, gx in zip(g45, mul(bf(g45), bf(x64)))]
    return mul(bf(g0123), bf(g456))


def _gdn_kernel(qkv_ref, z_ref, gate_ref, convw_ref, alog_ref, dtb_ref, gn_ref, o_ref, xc_ref, s_ref, *, nb):
    c = BLK
    nh = GDN_HEADS
    d = GDN_D
    w_qkv = nh * d * 3
    chains = [(bb, h) for bb in range(nb) for h in range(nh)]
    ns = range(len(chains))

    @pl.when(pl.program_id(1) == 0)
    def _():
        xc_ref[...] = jnp.zeros_like(xc_ref)
        s_ref[...] = jnp.zeros_like(s_ref)

    row = _iota2((c, c), 0)
    col = _iota2((c, c), 1)
    causal = row >= col
    strict = row > col
    qkv, beta_all, gc_all, gc_all_t = [], [], [], []
    for bb in range(nb):
        x = qkv_ref[bb]
        tail = xc_ref[bb]
        sub8 = _iota2((8, w_qkv), 0)
        conv = x * convw_ref[GDN_CONV - 1:GDN_CONV, :]
        for k in range(1, GDN_CONV):
            rolled = pltpu.roll(x, k, 0)
            head = jnp.where(sub8 < k, pltpu.roll(tail, k, 0), rolled[0:8])
            shifted = jnp.concatenate([head, rolled[8:]], axis=0)
            conv = conv + shifted * convw_ref[GDN_CONV - 1 - k:GDN_CONV - k, :]
        xc_ref[bb] = x[c - 8:c]
        qkv.append(_silu(conv))
        gate = gate_ref[bb]
        beta_all.append(_sigmoid(gate))
        g_all = -jnp.exp(alog_ref[...]) * _softplus(gate + dtb_ref[...])
        gc_all.append(_cumsum_rows(g_all))
        gc_all_t.append(gc_all[bb].T)

    def l2n(x):
        return x * lax.rsqrt(jnp.sum(x * x, axis=-1, keepdims=True) + L2_EPS)

    q = [l2n(qkv[bb][:, h * d:(h + 1) * d]) * (d ** -0.5) for bb, h in chains]
    k = [l2n(qkv[bb][:, (nh + h) * d:(nh + h + 1) * d]) for bb, h in chains]
    v = [qkv[bb][:, (2 * nh + h) * d:(2 * nh + h + 1) * d] for bb, h in chains]
    beta = [beta_all[bb][:, h:h + 1] for bb, h in chains]
    gc = [gc_all[bb][:, nh + h:nh + h + 1] for bb, h in chains]
    gc_last = [gc_all[bb][c - 1:c, nh + h:nh + h + 1] for bb, h in chains]
    gc_row = [gc_all_t[bb][nh + h:nh + h + 1, :] for bb, h in chains]
    decay = [jnp.where(causal, jnp.exp(jnp.minimum(gc[n] - gc_row[n], 0.0)), 0.0) for n in ns]
    kb = [k[n] * beta[n] for n in ns]
    kbf = [k[n].astype(BF16) for n in ns]
    kk = [_dot_nt(kb[n].astype(BF16), kbf[n]) for n in ns]
    qk = [_dot_nt(q[n].astype(BF16), kbf[n]) for n in ns]
    t_inv = _unit_lower_inverses([jnp.where(strict, kk[n] * decay[n], 0.0) for n in ns])
    t_invb = [t.astype(BF16) for t in t_inv]
    egc = [jnp.exp(gc[n]) for n in ns]
    u = [_dot(t_invb[n], (v[n] * beta[n]).astype(BF16)) for n in ns]
    w = [_dot(t_invb[n], (kb[n] * egc[n]).astype(BF16)).astype(BF16) for n in ns]
    a_intra = [(qk[n] * decay[n]).astype(BF16) for n in ns]
    q_dec = [(q[n] * egc[n]).astype(BF16) for n in ns]
    k_dec = [(k[n] * jnp.exp(gc_last[n] - gc[n])).astype(BF16) for n in ns]
    s = [s_ref[bb, h] for bb, h in chains]
    sb = [s[n].astype(BF16) for n in ns]
    v_new = [(u[n] - _dot(w[n], sb[n])).astype(BF16) for n in ns]
    o_state = [_dot(q_dec[n], sb[n]) for n in ns]
    o_intra = [_dot(a_intra[n], v_new[n]) for n in ns]
    s_add = [_dot_tn(k_dec[n], v_new[n]) for n in ns]
    for n, (bb, h) in enumerate(chains):
        s_ref[bb, h] = s[n] * jnp.exp(gc_last[n]) + s_add[n]
        o = o_state[n] + o_intra[n]
        o = o * lax.rsqrt(jnp.mean(o * o, axis=-1, keepdims=True) + RMS_EPS) * gn_ref[...]
        o_ref[bb, :, h * d:(h + 1) * d] = (o * _silu(z_ref[bb, :, h * d:(h + 1) * d])).astype(o_ref.dtype)


def _gdn(gdn_in, gate, conv_w, alog_row, dtb_row, gn_row, nb=GDN_BATCH_PER_STEP):
    b, t, _ = gdn_in.shape
    assert b % nb == 0
    nc = t // BLK
    w_qkv = GDN_HEADS * GDN_D * 3
    w_v = GDN_HEADS * GDN_D
    fixed = lambda i, j: (0, 0)
    return pl.pallas_call(
        functools.partial(_gdn_kernel, nb=nb),
        grid=(b // nb, nc),
        in_specs=[
            pl.BlockSpec((nb, BLK, w_qkv), lambda i, j: (i, j, 0)),
            pl.BlockSpec((nb, BLK, w_v), lambda i, j: (i, j, 3)),
            pl.BlockSpec((nb, BLK, BLK), lambda i, j: (i, j, 0)),
            pl.BlockSpec(conv_w.shape, fixed),
            pl.BlockSpec(alog_row.shape, fixed),
            pl.BlockSpec(dtb_row.shape, fixed),
            pl.BlockSpec(gn_row.shape, fixed),
        ],
        out_specs=pl.BlockSpec((nb, BLK, w_v), lambda i, j: (i, j, 0)),
        out_shape=jax.ShapeDtypeStruct((b, t, w_v), BF16),
        scratch_shapes=[
            pltpu.VMEM((nb, 8, w_qkv), F32),
            pltpu.VMEM((nb, GDN_HEADS, GDN_D, GDN_D), F32),
        ],
        compiler_params=_cparams(("parallel", "arbitrary")),
        name="gdn",
    )(gdn_in, gdn_in, gate, conv_w, alog_row, dtb_row, gn_row)


def _sb_query_block(qi, q_ref, k_ref, v_ref, o_ref, pairs):
    c = BLK
    rel = _iota2((2 * c, c), 1) - (_iota2((2 * c, c), 0) & (c - 1))
    upper_strict = (_iota2((c, c), 0) > _iota2((c, c), 1)).astype(BF16)
    upper_strict2 = jnp.concatenate([upper_strict, upper_strict], axis=0)
    head0 = _iota2((c, c), 1) < SB_DH
    scale = jnp.asarray(SB_DH ** -0.5, BF16)
    zero_b = jnp.zeros((c, c), BF16)

    def split_heads(x):
        return jnp.concatenate([jnp.where(head0, x, zero_b), jnp.where(head0, zero_b, x)], axis=0)

    q_cat = [split_heads(q_ref[:, p * c:(p + 1) * c] * scale) for p in range(pairs)]

    def cond(carry):
        return (carry[0] >= 0) & (carry[1] > 0)

    def body(carry):
        j0 = carry[0]
        rng = range(pairs)
        blocks = range(SB_KEY_BLOCKS)
        js = [j0 - i for i in blocks]
        starts = [pl.multiple_of(jnp.maximum(j, 0) * c, c) for j in js]
        diag_ok = rel < (qi - j0) * c
        zs = [[_dot_nt(q_cat[p], k_ref[pl.ds(starts[i], c), p * c:(p + 1) * c]) * LOG2E for p in rng] for i in blocks]
        neg_l1m = [[None] * pairs for _ in blocks]
        log_b = [[None] * pairs for _ in blocks]
        hilo = [[None] * pairs for _ in blocks]
        for i in blocks:
            for p in rng:
                t = jnp.log2(1.0 + jnp.exp2(-jnp.abs(zs[i][p])))
                nl = jnp.maximum(zs[i][p], 0.0) + t
                log_b[i][p] = zs[i][p] - nl
                if i == 0:
                    nl = jnp.where(diag_ok, nl, 0.0)
                neg_l1m[i][p] = nl
                hi = nl.astype(BF16)
                hilo[i][p] = jnp.concatenate([hi, (nl - hi.astype(F32)).astype(BF16)], axis=1)
        local = [[_dot(hilo[i][p], upper_strict2) for p in rng] for i in blocks]
        out = []
        best = None
        for p in rng:
            run = carry[3 + 2 * p]
            w_cat, v_cat = [], []
            for i in blocks:
                wgt = jnp.exp2(log_b[i][p] - local[i][p] - run)
                if i == 0:
                    wgt = jnp.where(diag_ok, wgt, 0.0)
                wgt = wgt.astype(BF16)
                w_cat += [wgt[:c], wgt[c:]]
                vj = v_ref[pl.ds(starts[i], c), p * c:(p + 1) * c]
                if i > 0:
                    vj = jnp.where(js[i] >= 0, vj, zero_b)
                v_cat.append(split_heads(vj))
                run = run + jnp.sum(neg_l1m[i][p], axis=-1, keepdims=True)
            acc = carry[2 + 2 * p] + _dot(jnp.concatenate(w_cat, axis=1), jnp.concatenate(v_cat, axis=0))
            out += [acc, run]
            best = run if best is None else jnp.minimum(best, run)
        live = (jnp.min(best) <= SB_UNDERFLOW_LOG2).astype(jnp.int32)
        return (j0 - SB_KEY_BLOCKS, live, *out)

    init = [qi, jnp.int32(1)]
    for p in range(pairs):
        init += [jnp.zeros((c, c), F32), jnp.zeros((2 * c, 1), F32)]
    res = lax.while_loop(cond, body, tuple(init))
    for p in range(pairs):
        o_ref[:, p * c:(p + 1) * c] = res[2 + 2 * p].astype(o_ref.dtype)


def _sb_kernel(q_ref, k_ref, v_ref, o_ref, *, pairs):
    for qq in range(SB_QUERY_BLOCKS):
        rows = pl.ds(qq * BLK, BLK)
        _sb_query_block(pl.program_id(1) * SB_QUERY_BLOCKS + qq, q_ref.at[rows], k_ref, v_ref, o_ref.at[rows], pairs)


def _stick_breaking(sb_in):
    b, t, _ = sb_in.shape
    rows = SB_QUERY_BLOCKS * BLK
    assert t % rows == 0
    pairs = SB_HEADS * SB_DH // BLK
    w = pairs * BLK
    return pl.pallas_call(
        functools.partial(_sb_kernel, pairs=pairs),
        grid=(b, t // rows),
        in_specs=[
            pl.BlockSpec((None, rows, w), lambda i, j: (i, j, 0)),
            pl.BlockSpec((None, t, w), lambda i, j: (i, 0, 1)),
            pl.BlockSpec((None, t, w), lambda i, j: (i, 0, 2)),
        ],
        out_specs=pl.BlockSpec((None, rows, w), lambda i, j: (i, j, 0)),
        out_shape=jax.ShapeDtypeStruct((b, t, w), BF16),
        compiler_params=_cparams(("parallel", "arbitrary")),
        name="stick_breaking",
    )(sb_in, sb_in, sb_in)


def _row_in_group(x, idx, group):
    c, d = x.shape
    x3 = x.reshape(c // group, group, d)
    return jnp.broadcast_to(x3[:, idx:idx + 1, :], x3.shape).reshape(c, d)


def _split_row(b, half):
    if half >= 8:
        return _row_in_group(b, half, 2 * half)
    sub = _iota2(b.shape, 0) & 7
    picks = [_row_in_group(b, s, 8) for s in range(half, 8, 2 * half)]
    out = picks[-1]
    for i in reversed(range(len(picks) - 1)):
        out = jnp.where(sub < (i + 1) * 2 * half, picks[i], out)
    return out


def _hgrn_intra(qs, ks, bs):
    c = qs[0].shape[0]
    hs = range(len(qs))
    row = _iota2((c, c), 0)
    col = _iota2((c, c), 1)
    levels = []
    half = c // 2
    while half >= 1:
        lower = (row & (2 * half - 1)) >= half
        scale = jnp.where(lower, LOG2E, -LOG2E)
        prods = []
        for h in hs:
            e = jnp.exp2((bs[h] - _split_row(bs[h], half)) * scale)
            ql = jnp.where(lower, qs[h] * e, 0.0).astype(BF16)
            kl = jnp.where(lower, 0.0, ks[h] * e).astype(BF16)
            prods.append(_dot_nt(ql, kl))
        levels.append((half, prods))
        half //= 2
    diag = [jnp.sum(qs[h] * ks[h], axis=-1, keepdims=True) for h in hs]
    out = []
    for h in hs:
        a = levels[0][1][h]
        for half, prods in levels[1:]:
            shift = int(math.log2(2 * half))
            a = jnp.where((row >> shift) == (col >> shift), prods[h], a)
        out.append(jnp.where(row == col, diag[h], a))
    return out


def _hgrn_kernel(q_ref, f_ref, i_ref, z_ref, lbraw_ref, gn_ref, o_ref, s_ref, *, pad, layer, heads):
    c = BLK
    d = HG_D
    j = pl.program_id(2)
    hs = range(heads)

    @pl.when(j == 0)
    def _():
        s_ref[...] = jnp.zeros_like(s_ref)

    real = (j * c + _iota2((c, d), 0)) >= pad
    raw = lbraw_ref[...]
    ex = jnp.exp(raw - jnp.max(raw, axis=0, keepdims=True))
    sm = ex / jnp.sum(ex, axis=0, keepdims=True)
    lb_all = jnp.sum(sm[1:layer + 1, :], axis=0, keepdims=True)

    sl = [slice(h * d, (h + 1) * d) for h in hs]
    lb = [lb_all[:, sl[h]] for h in hs]
    q = [_silu(q_ref[:, sl[h]]) for h in hs]
    fgate = [lb[h] + (1.0 - lb[h]) * _sigmoid(f_ref[:, sl[h]]) for h in hs]
    k = [1.0 - fgate[h] for h in hs]
    vb = [jnp.where(real, i_ref[:, sl[h]], 0.0).astype(BF16) for h in hs]
    b = [_cumsum_rows(jnp.log(fgate[h])) for h in hs]
    b_last = [b[h][c - 1:c, :] for h in hs]
    s_t = [s_ref[h] for h in hs]
    o_state = [_dot_nt((q[h] * jnp.exp(b[h])).astype(BF16), s_t[h].astype(BF16)) for h in hs]
    a = _hgrn_intra(q, k, b)
    o_intra = [_dot(a[h].astype(BF16), vb[h]) for h in hs]
    k_dec = [(k[h] * jnp.exp(b_last[h] - b[h])).astype(BF16) for h in hs]
    s_add = [_dot_tn(vb[h], k_dec[h]) for h in hs]
    for h in hs:
        s_ref[h] = s_t[h] * jnp.exp(b_last[h]) + s_add[h]
        o = o_state[h] + o_intra[h]
        o = o * lax.rsqrt(jnp.mean(o * o, axis=-1, keepdims=True) + RMS_EPS) * gn_ref[...]
        o_ref[:, sl[h]] = (o * _silu(z_ref[:, sl[h]])).astype(o_ref.dtype)


def _hgrn2(c_in, lb_raw, gn_row, pad, layer, heads_per_step=8):
    b, t, _ = c_in.shape
    nc = t // BLK
    groups = HG_HEADS // heads_per_step
    w = heads_per_step * HG_D
    depth = lb_raw.shape[0]
    col_spec = lambda part: pl.BlockSpec((None, BLK, w), lambda i, g, j: (i, j, part * groups + g))
    return pl.pallas_call(
        functools.partial(_hgrn_kernel, pad=pad, layer=layer, heads=heads_per_step),
        grid=(b, groups, nc),
        in_specs=[
            col_spec(0), col_spec(1), col_spec(2), col_spec(3),
            pl.BlockSpec((depth, w), lambda i, g, j: (0, g)),
            pl.BlockSpec(gn_row.shape, lambda i, g, j: (0, 0)),
        ],
        out_specs=pl.BlockSpec((None, BLK, w), lambda i, g, j: (i, j, g)),
        out_shape=jax.ShapeDtypeStruct((b, t, HG_HEADS * HG_D), BF16),
        scratch_shapes=[pltpu.VMEM((heads_per_step, HG_D, HG_D), F32)],
        compiler_params=_cparams(("parallel", "parallel", "arbitrary")),
        name="hgrn2",
    )(c_in, c_in, c_in, c_in, lb_raw, gn_row)


def _lane_row(vec, offset, width=BLK):
    return jnp.zeros((1, width), F32).at[0, offset:offset + vec.shape[0]].set(vec.astype(F32))


def kernel(x, meta_tokens, ab_w_in, ab_conv_w, ab_a_log, ab_dt_bias, ab_gnorm_g, ab_w_out, c_w_in, c_lb_raw,
           c_gnorm_g, c_w_out, ln_mix_g, ln_mix_b, mlp_w1, mlp_w2, ln_ffn_g, ln_ffn_b):
    bsz, seq, d = x.shape
    n_meta = meta_tokens.shape[0]
    depth = ln_mix_g.shape[0]
    alpha = float((2 * depth) ** 0.25)
    pad = (-n_meta) % BLK
    t = pad + n_meta + seq
    assert seq % BLK == 0 and depth == 2
    m = bsz * t
    tm = 512
    assert m % tm == 0

    meta = jnp.broadcast_to(meta_tokens[None].astype(x.dtype), (bsz, n_meta, d))
    h = jnp.concatenate([jnp.zeros((bsz, pad, d), x.dtype), meta, x], axis=1).reshape(m, d)

    w_gdn = GDN_HEADS * GDN_D * 4
    w_sb = SB_HEADS * SB_DH * 3
    row2 = lambda a: a.astype(F32).reshape(1, -1)

    w_in = ab_w_in[0]
    wa = w_in[:, :w_gdn].astype(BF16)
    wg = jnp.pad(w_in[:, w_gdn:w_gdn + 2 * GDN_HEADS], ((0, 0), (0, BLK - 2 * GDN_HEADS))).astype(BF16)
    wb = w_in[:, w_gdn + 2 * GDN_HEADS:].astype(BF16)
    gdn_in, gate, sb_in = _project(h, [wa, wg, wb], [F32, F32, BF16], tm)
    oa = _gdn(gdn_in.reshape(bsz, t, w_gdn), gate.reshape(bsz, t, BLK), ab_conv_w[0].astype(F32),
              _lane_row(ab_a_log[0], GDN_HEADS), _lane_row(ab_dt_bias[0], GDN_HEADS), row2(ab_gnorm_g[0]))
    ob = _stick_breaking(sb_in.reshape(bsz, t, w_sb))
    w_out = ab_w_out[0].astype(BF16)
    n_a = GDN_HEADS * GDN_D
    h = _post_mixer([oa.reshape(m, n_a), ob.reshape(m, -1)], [w_out[:n_a], w_out[n_a:]], h,
                    row2(ln_mix_g[0]), row2(ln_mix_b[0]), mlp_w1[0].astype(BF16), mlp_w2[0].astype(BF16),
                    row2(ln_ffn_g[0]), row2(ln_ffn_b[0]), alpha, tm)

    (c_in,) = _project(h, [c_w_in[0].astype(BF16)], [F32], tm)
    oc = _hgrn2(c_in.reshape(bsz, t, -1), c_lb_raw.astype(F32), row2(c_gnorm_g[0]), pad, 1)
    out = _post_mixer([oc.reshape(m, -1)], [c_w_out[0].astype(BF16)], h,
                      row2(ln_mix_g[1]), row2(ln_mix_b[1]), mlp_w1[1].astype(BF16), mlp_w2[1].astype(BF16),
                      row2(ln_ffn_g[1]), row2(ln_ffn_b[1]), alpha, tm, keep=(bsz, t, pad + n_meta, seq))
    return out.reshape(bsz, seq, d)
```
